```python
import jax
import jax.numpy as jnp
from jax import lax
import numpy as np

D_MODEL = 1024
BATCH = 1
SEQ = 16384
DEPTH = 2
DEC_BATCH = 32
DEC_SEQ = 1
PAST_LEN = 16384
PAGE_SIZE = 128

N_HEADS = 8
N_KV_HEADS = 2
GROUP = N_HEADS // N_KV_HEADS
HEAD_DIM = 64
NSA_WIDTH = N_HEADS * HEAD_DIM
SCALE = HEAD_DIM ** -0.5
CMP_STRIDE = 16
CMP_LEN = 2 * CMP_STRIDE
SLC_BLOCK = 64
N_SEL_BLOCKS = 16
WINDOW = 512
Q_BLOCK = 128
GMLP_GROUPS = 8
GMLP_GROUP_DIM = 64
GMLP_WIDTH = GMLP_GROUPS * GMLP_GROUP_DIM
CHUNK = 128
MIX_WIDTH = NSA_WIDTH + GMLP_WIDTH
KV_COLS = N_KV_HEADS * 2 * HEAD_DIM
N_GATES = 3 * N_HEADS
PROJ_WIDTH = NSA_WIDTH + 3 * KV_COLS + N_GATES + 2 * GMLP_WIDTH
D_FF = 2816
N_EXPERTS = 8
TOP_K = 2
D_FF_EXPERT = 2816
MOE_BLOCK = 128
N_DENSE = (DEPTH + 1) // 2
N_MOE = DEPTH // 2
DEEPNORM_ALPHA = (2 * DEPTH) ** 0.25
DEEPNORM_BETA = (8 * DEPTH) ** -0.25
LN_EPS = 1e-5

kernel_name = 'nsa_gmlp_hybrid_deepnorm_step'


def alibi_slopes():
    h = jnp.arange(1, N_HEADS + 1, dtype=jnp.float32)
    return jnp.exp2(-8.0 * h / N_HEADS).reshape(N_KV_HEADS, GROUP)


def layer_norm(x, g, b):
    xf = x.astype(jnp.float32)
    xc = xf - jnp.mean(xf, axis=-1, keepdims=True)
    var = jnp.mean(xc * xc, axis=-1, keepdims=True)
    return (xc * lax.rsqrt(var + LN_EPS) * g + b).astype(x.dtype)


def masked_softmax(s, mask):
    s = jnp.where(mask, s, -jnp.inf)
    m = jnp.max(s, axis=-1, keepdims=True)
    m = jnp.where(jnp.isfinite(m), m, 0.0)
    p = jnp.exp(s - m)
    return p / jnp.maximum(jnp.sum(p, axis=-1, keepdims=True), 1e-30)


def split_proj(z):
    B, T, _ = z.shape
    cuts = [NSA_WIDTH]
    for w in (KV_COLS, KV_COLS, KV_COLS, N_GATES, GMLP_WIDTH):
        cuts.append(cuts[-1] + w)
    q, kvc, kvs, kvw, g, u, v = jnp.split(z, cuts, axis=-1)
    kv_shape = (B, T, N_KV_HEADS, 2, HEAD_DIM)
    return (q.reshape(B, T, N_KV_HEADS, GROUP, HEAD_DIM),
            kvc.reshape(kv_shape), kvs.reshape(kv_shape), kvw.reshape(kv_shape),
            jax.nn.sigmoid(g).reshape(B, T, 3, N_KV_HEADS, GROUP),
            jax.nn.gelu(u), jax.nn.gelu(v))


def sub_block_proj(sub, w1):
    a = jnp.einsum('bnlkcd,cldf->bnkcf', sub, w1[:, :CMP_STRIDE])
    c = jnp.einsum('bnlkcd,cldf->bnkcf', sub, w1[:, CMP_STRIDE:])
    return a, c


def finish_compress(a, c, w2):
    h = jax.nn.gelu(a + jnp.concatenate([c[:, 1:], jnp.zeros_like(c[:, :1])], axis=1))
    out = jnp.einsum('bnkcf,cfe->bnkce', h, w2)
    return out[..., 0, :], out[..., 1, :]


def cmp_branch(qg, qpos, kc, vc, slopes):
    n_cmp = kc.shape[1]
    cpos = jnp.arange(n_cmp) * CMP_STRIDE + (CMP_LEN - 1)
    dist = qpos[:, None] - cpos[None, :]
    s = jnp.einsum('bqkgd,bnkd->bkgqn', qg, kc).astype(jnp.float32) * SCALE
    s = s - slopes[None, :, :, None, None] * dist.astype(jnp.float32)
    p = masked_softmax(s, dist >= 0)
    return jnp.einsum('bkgqn,bnkd->bqkgd', p, vc), p


def slc_branch(qg, qpos, imp, fetch, n_sel, slopes):
    B, Qb = qg.shape[:2]
    per = SLC_BLOCK // CMP_STRIDE
    imp = jnp.pad(imp, ((0, 0), (0, 0), (0, 0), (0, n_sel * per - imp.shape[-1])))
    imp = imp.reshape(B, N_KV_HEADS, Qb, n_sel, per).sum(axis=-1)
    blk = jnp.arange(n_sel)[None, :]
    cur = (qpos // SLC_BLOCK)[:, None]
    forced = (blk == 0) | (blk == cur) | (blk == cur - 1)
    score = jnp.where(blk <= cur, jnp.where(forced, jnp.inf, imp), -jnp.inf)
    k_top = min(N_SEL_BLOCKS, n_sel)
    top_s, idx = lax.top_k(score, k_top)
    kpos = idx[..., None] * SLC_BLOCK + jnp.arange(SLC_BLOCK)
    k_sel, v_sel = fetch(kpos)
    dist = qpos[None, None, :, None, None] - kpos
    s = jnp.einsum('bqkgd,bkqnsd->bkgqns', qg, k_sel).astype(jnp.float32) * SCALE
    s = s - slopes[None, :, :, None, None, None] * dist[:, :, None].astype(jnp.float32)
    mask = (top_s[..., None] > -jnp.inf) & (dist >= 0)
    n_keys = k_top * SLC_BLOCK
    p = masked_softmax(s.reshape(B, N_KV_HEADS, GROUP, Qb, n_keys),
                       mask.reshape(B, N_KV_HEADS, 1, Qb, n_keys))
    return jnp.einsum('bkgqm,bkqmd->bqkgd', p, v_sel.reshape(B, N_KV_HEADS, Qb, n_keys, HEAD_DIM))


def win_branch(qg, qpos, kw, kpos, slopes):
    dist = qpos[:, None] - kpos[None, :]
    s = jnp.einsum('bqkgd,blkd->bkgql', qg, kw[..., 0, :]).astype(jnp.float32) * SCALE
    s = s - slopes[None, :, :, None, None] * dist.astype(jnp.float32)
    mask = (dist >= 0) & (dist <= WINDOW) & (kpos >= 0)[None, :]
    p = masked_softmax(s, mask)
    return jnp.einsum('bkgql,blkd->bqkgd', p, kw[..., 1, :])


def nsa_core(qg, qpos, gates, kc, vc, fetch, n_sel, kw, kpos):
    slopes = alibi_slopes()
    o_c, p_c = cmp_branch(qg, qpos, kc, vc, slopes)
    o_s = slc_branch(qg, qpos, p_c.sum(axis=2), fetch, n_sel, slopes)
    o_w = win_branch(qg, qpos, kw, kpos, slopes)
    g = gates.astype(jnp.float32)[..., None]
    o = g[:, :, 0] * o_c + g[:, :, 1] * o_s + g[:, :, 2] * o_w
    B, Qb = qg.shape[:2]
    return o.reshape(B, Qb, NSA_WIDTH).astype(qg.dtype)


def make_fetch_contiguous(kvs, n_sel):
    B, T = kvs.shape[:2]
    kv = jnp.pad(kvs, ((0, 0), (0, n_sel * SLC_BLOCK - T), (0, 0), (0, 0), (0, 0)))
    bi = jnp.arange(B)[:, None, None, None, None]
    hi = jnp.arange(N_KV_HEADS)[None, :, None, None, None]

    def fetch(kpos):
        g = kv[bi, kpos, hi]
        return g[..., 0, :], g[..., 1, :]
    return fetch


def make_fetch_paged(cache_slc, l, page_table, kvs_new, t_past):
    B, Tn = kvs_new.shape[:2]
    n_new = -(-Tn // SLC_BLOCK) * SLC_BLOCK
    new = jnp.pad(kvs_new, ((0, 0), (0, n_new - Tn), (0, 0), (0, 0), (0, 0)))
    bi = jnp.arange(B)[:, None, None, None, None]
    hi = jnp.arange(N_KV_HEADS)[None, :, None, None, None]

    def fetch(kpos):
        kp = jnp.minimum(kpos, t_past - 1)
        page = page_table[bi, kp // PAGE_SIZE]
        g_past = cache_slc[l, page, kp % PAGE_SIZE, hi]
        g_new = new[bi, jnp.maximum(kpos - t_past, 0), hi]
        g = jnp.where((kpos < t_past)[..., None, None], g_past, g_new)
        return g[..., 0, :], g[..., 1, :]
    return fetch


def nsa_prompt(qg, kvc, kvs, kvw, gates, w_phi1, w_phi2):
    B, T = qg.shape[:2]
    n_sub = T // CMP_STRIDE
    a, c = sub_block_proj(kvc[:, :n_sub * CMP_STRIDE].reshape(
        B, n_sub, CMP_STRIDE, N_KV_HEADS, 2, HEAD_DIM), w_phi1)
    kc, vc = finish_compress(a, c, w_phi2)
    n_sel = -(-T // SLC_BLOCK)
    fetch = make_fetch_contiguous(kvs, n_sel)
    win = jnp.pad(kvw, ((0, 0), (WINDOW, 0), (0, 0), (0, 0), (0, 0)))
    qb = min(Q_BLOCK, T)
    nq = T // qb
    q_blocks = jnp.moveaxis(qg.reshape(B, nq, qb, N_KV_HEADS, GROUP, HEAD_DIM), 1, 0)
    g_blocks = jnp.moveaxis(gates.reshape(B, nq, qb, 3, N_KV_HEADS, GROUP), 1, 0)

    def block(args):
        q_i, g_i, i = args
        q0 = i * qb
        qpos = q0 + jnp.arange(qb)
        kw = lax.dynamic_slice_in_dim(win, q0, WINDOW + qb, axis=1)
        kpos = q0 - WINDOW + jnp.arange(WINDOW + qb)
        return nsa_core(q_i, qpos, g_i, kc, vc, fetch, n_sel, kw, kpos)

    out = lax.map(block, (q_blocks, g_blocks, jnp.arange(nq)))
    return jnp.moveaxis(out, 0, 1).reshape(B, T, NSA_WIDTH)


def nsa_sample(qg, kvc, kvs, kvw, gates, cache_cmp, cache_slc, cache_win, page_table, l,
               w_phi1, w_phi2):
    B, Tn = qg.shape[:2]
    t_past = page_table.shape[1] * PAGE_SIZE
    past = cache_cmp[l, page_table].reshape(
        B, t_past // CMP_STRIDE, CMP_STRIDE, N_KV_HEADS, 2, HEAD_DIM)
    a_p, c_p = sub_block_proj(past, w_phi1)
    n_new_sub = Tn // CMP_STRIDE
    a_n, c_n = sub_block_proj(kvc[:, :n_new_sub * CMP_STRIDE].reshape(
        B, n_new_sub, CMP_STRIDE, N_KV_HEADS, 2, HEAD_DIM), w_phi1)
    kc, vc = finish_compress(jnp.concatenate([a_p, a_n], axis=1),
                             jnp.concatenate([c_p, c_n], axis=1), w_phi2)
    n_sel = -(-(t_past + Tn) // SLC_BLOCK)
    fetch = make_fetch_paged(cache_slc, l, page_table, kvs, t_past)
    lb = cache_win.shape[2]
    kw = jnp.concatenate([cache_win[l], kvw.astype(cache_win.dtype)], axis=1)
    kpos = t_past - lb + jnp.arange(lb + Tn)
    qpos = t_past + jnp.arange(Tn)
    return nsa_core(qg, qpos, gates, kc, vc, fetch, n_sel, kw, kpos)


def gmlp_mixer(u, v, ln_g, ln_b, ws, bs):
    B, T, _ = u.shape
    L = min(CHUNK, T)
    vn = layer_norm(v, ln_g, ln_b)
    vc = vn.reshape(B, T // L, L, GMLP_GROUPS, GMLP_GROUP_DIM)
    w = jnp.where(jnp.tril(jnp.ones((L, L), dtype=bool)), ws[:, :L, :L], 0.0)
    mix = jnp.einsum('gts,bcsgd->bctgd', w, vc) + bs[:, :L].T[None, None, :, :, None]
    out = u.reshape(B, T // L, L, GMLP_GROUPS, GMLP_GROUP_DIM) * mix
    return out.reshape(B, T, GMLP_WIDTH), vn


def swiglu(x, w1, w2):
    g, u = jnp.split(x @ w1, 2, axis=-1)
    return (jax.nn.silu(g) * u) @ w2


def moe_ffn(x, w_router, w1, w2):
    B, T, D = x.shape
    xt = x.reshape(B * T, D)
    N = xt.shape[0]
    logits = (xt @ w_router).astype(jnp.float32)
    top_l, top_e = lax.top_k(logits, TOP_K)
    top_w = jax.nn.softmax(top_l, axis=-1)
    A = N * TOP_K
    e_flat = top_e.reshape(A)
    t_flat = jnp.repeat(jnp.arange(N, dtype=jnp.int32), TOP_K)
    w_flat = top_w.reshape(A)
    order = jnp.argsort(e_flat)
    e_sorted = e_flat[order]
    counts = jnp.zeros((N_EXPERTS,), jnp.int32).at[e_flat].add(1)
    padded = (counts + MOE_BLOCK - 1) // MOE_BLOCK * MOE_BLOCK
    pad_end = jnp.cumsum(padded)
    pad_start = pad_end - padded
    start = jnp.cumsum(counts) - counts
    dest = pad_start[e_sorted] + jnp.arange(A, dtype=jnp.int32) - start[e_sorted]
    n_blk = (A + N_EXPERTS * (MOE_BLOCK - 1) + MOE_BLOCK - 1) // MOE_BLOCK
    n_slot = n_blk * MOE_BLOCK
    slot_tok = jnp.zeros((n_slot,), jnp.int32).at[dest].set(t_flat[order])
    slot_w = jnp.zeros((n_slot,), jnp.float32).at[dest].set(w_flat[order])
    blk_e = jnp.minimum(jnp.searchsorted(pad_end, jnp.arange(n_blk, dtype=jnp.int32) * MOE_BLOCK,
                                         side='right'), N_EXPERTS - 1)
    xb = xt[slot_tok].reshape(n_blk, MOE_BLOCK, D)

    def expert_block(args):
        xi, e = args
        return swiglu(xi, w1[e], w2[e])

    yb = lax.map(expert_block, (xb, blk_e)).reshape(n_slot, D)
    y = jnp.zeros((N, D), jnp.float32).at[slot_tok].add(yb.astype(jnp.float32) * slot_w[:, None])
    return y.astype(x.dtype).reshape(B, T, D)


def channel_mix(x, l, ffn_w1, ffn_w2, moe_router, moe_w1, moe_w2):
    i = l // 2
    if l % 2 == 0:
        return swiglu(x, ffn_w1[i], ffn_w2[i])
    return moe_ffn(x, moe_router[i], moe_w1[i], moe_w2[i])


def setup_inputs(seed: int = 0) -> dict:
    key = jax.random.key(seed)
    ks = jax.random.split(key, 23)
    f32 = jnp.float32
    n_pages = PAST_LEN // PAGE_SIZE
    n_pool = (5 * DEC_BATCH * n_pages + 3) // 4
    win_len = min(WINDOW, PAST_LEN)

    def nrm(k, shape, scale):
        return jax.random.normal(k, shape, f32) * scale

    page_table = jax.random.permutation(ks[5], n_pool)[:DEC_BATCH * n_pages]
    return {
        'x_prompt': nrm(ks[0], (BATCH, SEQ, D_MODEL), 1.0),
        'x_sample': nrm(ks[1], (DEC_BATCH, DEC_SEQ, D_MODEL), 1.0),
        'cache_cmp': nrm(ks[2], (DEPTH, n_pool, PAGE_SIZE, N_KV_HEADS, 2, HEAD_DIM), 1.0),
        'cache_slc': nrm(ks[3], (DEPTH, n_pool, PAGE_SIZE, N_KV_HEADS, 2, HEAD_DIM), 1.0),
        'cache_win': nrm(ks[4], (DEPTH, DEC_BATCH, win_len, N_KV_HEADS, 2, HEAD_DIM), 1.0),
        'page_table': page_table.reshape(DEC_BATCH, n_pages).astype(jnp.int32),
        'w_in': nrm(ks[6], (DEPTH, D_MODEL, PROJ_WIDTH), D_MODEL ** -0.5),
        'w_phi1': nrm(ks[7], (DEPTH, 2, CMP_LEN, HEAD_DIM, HEAD_DIM), (CMP_LEN * HEAD_DIM) ** -0.5),
        'w_phi2': nrm(ks[8], (DEPTH, 2, HEAD_DIM, HEAD_DIM), HEAD_DIM ** -0.5),
        'gmlp_ln_g': 1.0 + nrm(ks[9], (DEPTH, GMLP_WIDTH), 0.05),
        'gmlp_ln_b': nrm(ks[10], (DEPTH, GMLP_WIDTH), 0.05),
        'gmlp_ws': nrm(ks[11], (DEPTH, GMLP_GROUPS, CHUNK, CHUNK), CHUNK ** -0.5),
        'gmlp_bs': 1.0 + nrm(ks[12], (DEPTH, GMLP_GROUPS, CHUNK), 0.1),
        'w_out': nrm(ks[13], (DEPTH, MIX_WIDTH, D_MODEL), DEEPNORM_BETA * MIX_WIDTH ** -0.5),
        'ln1_g': 1.0 + nrm(ks[14], (DEPTH, D_MODEL), 0.05),
        'ln1_b': nrm(ks[15], (DEPTH, D_MODEL), 0.05),
        'ln2_g': 1.0 + nrm(ks[16], (DEPTH, D_MODEL), 0.05),
        'ln2_b': nrm(ks[17], (DEPTH, D_MODEL), 0.05),
        'ffn_w1': nrm(ks[18], (N_DENSE, D_MODEL, 2 * D_FF), D_MODEL ** -0.5),
        'ffn_w2': nrm(ks[19], (N_DENSE, D_FF, D_MODEL), DEEPNORM_BETA * D_FF ** -0.5),
        'moe_router': nrm(ks[20], (N_MOE, D_MODEL, N_EXPERTS), D_MODEL ** -0.5),
        'moe_w1': nrm(ks[21], (N_MOE, N_EXPERTS, D_MODEL, 2 * D_FF_EXPERT), D_MODEL ** -0.5),
        'moe_w2': nrm(ks[22], (N_MOE, N_EXPERTS, D_FF_EXPERT, D_MODEL),
                      DEEPNORM_BETA * D_FF_EXPERT ** -0.5),
    }


def reference(x_prompt, x_sample, cache_cmp, cache_slc, cache_win, page_table,
              w_in, w_phi1, w_phi2, gmlp_ln_g, gmlp_ln_b, gmlp_ws, gmlp_bs, w_out,
              ln1_g, ln1_b, ln2_g, ln2_b, ffn_w1, ffn_w2, moe_router, moe_w1, moe_w2):
    xp, xs = x_prompt, x_sample
    new_cmp_p, new_slc_p, new_win_p = [], [], []
    new_cmp_s, new_slc_s, new_win_s, new_gv_s = [], [], [], []
    for l in range(DEPTH):
        qg, kvc, kvs, kvw, gates, u, v = split_proj(xp @ w_in[l])
        o_attn = nsa_prompt(qg, kvc, kvs, kvw, gates, w_phi1[l], w_phi2[l])
        o_gmlp, _ = gmlp_mixer(u, v, gmlp_ln_g[l], gmlp_ln_b[l], gmlp_ws[l], gmlp_bs[l])
        mixed = jnp.concatenate([o_attn, o_gmlp.astype(o_attn.dtype)], axis=-1) @ w_out[l]
        xp = layer_norm(DEEPNORM_ALPHA * xp + mixed.astype(xp.dtype), ln1_g[l], ln1_b[l])
        n_win = min(WINDOW, kvw.shape[1])
        new_cmp_p.append(kvc)
        new_slc_p.append(kvs)
        new_win_p.append(kvw[:, kvw.shape[1] - n_win:])
        qg, kvc, kvs, kvw, gates, u, v = split_proj(xs @ w_in[l])
        o_attn = nsa_sample(qg, kvc, kvs, kvw, gates, cache_cmp, cache_slc, cache_win,
                            page_table, l, w_phi1[l], w_phi2[l])
        o_gmlp, v_norm = gmlp_mixer(u, v, gmlp_ln_g[l], gmlp_ln_b[l], gmlp_ws[l], gmlp_bs[l])
        mixed = jnp.concatenate([o_attn, o_gmlp.astype(o_attn.dtype)], axis=-1) @ w_out[l]
        xs = layer_norm(DEEPNORM_ALPHA * xs + mixed.astype(xs.dtype), ln1_g[l], ln1_b[l])
        new_cmp_s.append(kvc)
        new_slc_s.append(kvs)
        new_win_s.append(kvw)
        new_gv_s.append(v_norm)
        fp = channel_mix(xp, l, ffn_w1, ffn_w2, moe_router, moe_w1, moe_w2)
        xp = layer_norm(DEEPNORM_ALPHA * xp + fp.astype(xp.dtype), ln2_g[l], ln2_b[l])
        fs = channel_mix(xs, l, ffn_w1, ffn_w2, moe_router, moe_w1, moe_w2)
        xs = layer_norm(DEEPNORM_ALPHA * xs + fs.astype(xs.dtype), ln2_g[l], ln2_b[l])
    return (xp, xs, jnp.stack(new_cmp_p), jnp.stack(new_slc_p), jnp.stack(new_win_p),
            jnp.stack(new_cmp_s), jnp.stack(new_slc_s), jnp.stack(new_win_s), jnp.stack(new_gv_s))
```

```python
import functools

import jax
import jax.numpy as jnp
from jax import lax
from jax.experimental import pallas as pl
from jax.experimental.pallas import tpu as pltpu

F32 = jnp.float32
BF16 = jnp.bfloat16

D_MODEL = 1024
N_HEADS = 8
N_KV_HEADS = 2
GROUP = N_HEADS // N_KV_HEADS
HEAD_DIM = 64
NSA_WIDTH = N_HEADS * HEAD_DIM
SCALE = HEAD_DIM ** -0.5
CMP_STRIDE = 16
CMP_LEN = 32
SLC_BLOCK = 64
N_SEL_BLOCKS = 16
WINDOW = 512
GMLP_GROUPS = 8
GMLP_GROUP_DIM = 64
GMLP_WIDTH = GMLP_GROUPS * GMLP_GROUP_DIM
CHUNK = 128
KV_COLS = N_KV_HEADS * 2 * HEAD_DIM
N_GATES = 3 * N_HEADS
D_FF = 2816
N_EXPERTS = 8
PAGE_SIZE = 128
LN_EPS = 1e-5

LANES = 128
VMEM_LIMIT = 56 * 1024 * 1024


def _cparams(sem, vmem=VMEM_LIMIT):
    return pltpu.CompilerParams(dimension_semantics=sem, vmem_limit_bytes=vmem)


def _ln(x, g, b):
    xc = x - jnp.mean(x, axis=-1, keepdims=True)
    var = jnp.mean(xc * xc, axis=-1, keepdims=True)
    return xc * lax.rsqrt(var + LN_EPS) * g + b


def _dot(a, b):
    return jnp.dot(a, b, preferred_element_type=F32)


def _dot_nt(a, b):
    return lax.dot_general(a, b, (((1,), (1,)), ((), ())), preferred_element_type=F32)


def _full(shape):
    n = len(shape)
    return pl.BlockSpec(shape, lambda *_: (0,) * n)


def _inproj_kernel(x_ref, wq_ref, wkvt_ref, wkvc_ref, wg_ref, wu_ref, wv_ref, lng_ref, lnb_ref,
                   gw_ref, gb_ref, q_ref, kvt_ref, kvtb_ref, kvc_ref, gates_ref, og_ref, vn_ref, *,
                   sample):
    x = x_ref[...].astype(BF16)
    q_ref[...] = _dot(x, wq_ref[...]).astype(q_ref.dtype)
    kvt = _dot_nt(wkvt_ref[...], x)
    kvt_ref[...] = kvt
    kvtb_ref[...] = kvt[KV_COLS:].astype(kvtb_ref.dtype)
    kvc_ref[...] = _dot(x, wkvc_ref[...]).astype(kvc_ref.dtype)
    gates_ref[...] = jax.nn.sigmoid(_dot(x, wg_ref[...]))
    u = jax.nn.gelu(_dot(x, wu_ref[...]))
    v = jax.nn.gelu(_dot(x, wv_ref[...]))
    vn = _ln(v, lng_ref[...], lnb_ref[...])
    vn_ref[...] = vn
    if sample:
        og_ref[...] = (u * (vn * gw_ref[...] + gb_ref[...])).astype(og_ref.dtype)
    else:
        tm = x.shape[0]
        lane = lax.broadcasted_iota(jnp.int32, (CHUNK, LANES), 1)
        first = lane < GMLP_GROUP_DIM
        vb = vn.astype(BF16)
        for c in range(tm // CHUNK):
            rows = slice(c * CHUNK, (c + 1) * CHUNK)
            for p in range(GMLP_WIDTH // LANES):
                cols = slice(p * LANES, (p + 1) * LANES)
                vp = vb[rows, cols]
                mix = jnp.where(first, _dot(gw_ref[2 * p], vp), _dot(gw_ref[2 * p + 1], vp))
                og_ref[rows, cols] = (u[rows, cols] * (mix + gb_ref[p])).astype(og_ref.dtype)


def _inproj(x, w, *, sample, tm):
    rows = x.shape[0]
    grid = (rows // tm,)
    row = lambda n: pl.BlockSpec((tm, n), lambda i: (i, 0))
    if sample:
        gw_spec, gb_spec = _full((1, GMLP_WIDTH)), _full((1, GMLP_WIDTH))
        gw, gb = w["g_diag_w"], w["g_diag_b"]
    else:
        gw_spec, gb_spec = _full((GMLP_GROUPS, CHUNK, CHUNK)), _full((GMLP_WIDTH // LANES, CHUNK, LANES))
        gw, gb = w["g_tril"], w["g_bias"]
    out_shape = (
        jax.ShapeDtypeStruct((rows, NSA_WIDTH), BF16),
        jax.ShapeDtypeStruct((3 * KV_COLS, rows), F32),
        jax.ShapeDtypeStruct((2 * KV_COLS, rows), BF16),
        jax.ShapeDtypeStruct((rows, KV_COLS), BF16),
        jax.ShapeDtypeStruct((rows, LANES), F32),
        jax.ShapeDtypeStruct((rows, GMLP_WIDTH), BF16),
        jax.ShapeDtypeStruct((rows, GMLP_WIDTH), F32),
    )
    return pl.pallas_call(
        functools.partial(_inproj_kernel, sample=sample),
        grid=grid,
        in_specs=[row(D_MODEL), _full((D_MODEL, NSA_WIDTH)), _full((3 * KV_COLS, D_MODEL)),
                  _full((D_MODEL, KV_COLS)), _full((D_MODEL, LANES)), _full((D_MODEL, GMLP_WIDTH)),
                  _full((D_MODEL, GMLP_WIDTH)), _full((1, GMLP_WIDTH)), _full((1, GMLP_WIDTH)),
                  gw_spec, gb_spec],
        out_specs=(row(NSA_WIDTH), pl.BlockSpec((3 * KV_COLS, tm), lambda i: (0, i)),
                   pl.BlockSpec((2 * KV_COLS, tm), lambda i: (0, i)), row(KV_COLS),
                   row(LANES), row(GMLP_WIDTH), row(GMLP_WIDTH)),
        out_shape=out_shape,
        compiler_params=_cparams(("parallel",)),
        name="inproj_sample" if sample else "inproj_prompt",
    )(x, w["wq"], w["wkvt"], w["wkvc"], w["wg"], w["wu"], w["wv"], w["lng"], w["lnb"], gw, gb)


def _outproj_kernel(oa_ref, og_ref, x_ref, wa_ref, wg_ref, g_ref, b_ref, o_ref, *, alpha):
    mixed = _dot(oa_ref[...], wa_ref[...]) + _dot(og_ref[...], wg_ref[...])
    o_ref[...] = _ln(alpha * x_ref[...] + mixed, g_ref[...], b_ref[...])


def _outproj(oa, og, x, w, *, alpha, tm, name):
    rows = x.shape[0]
    row = lambda n: pl.BlockSpec((tm, n), lambda i: (i, 0))
    return pl.pallas_call(
        functools.partial(_outproj_kernel, alpha=alpha),
        grid=(rows // tm,),
        in_specs=[row(NSA_WIDTH), row(GMLP_WIDTH), row(D_MODEL), _full((NSA_WIDTH, D_MODEL)),
                  _full((GMLP_WIDTH, D_MODEL)), _full((1, D_MODEL)), _full((1, D_MODEL))],
        out_specs=row(D_MODEL),
        out_shape=jax.ShapeDtypeStruct((rows, D_MODEL), F32),
        compiler_params=_cparams(("parallel",)),
        name=name,
    )(oa, og, x, w["wo_a"], w["wo_g"], w["ln1_g"], w["ln1_b"])


def _ffn_kernel(x_ref, w1_ref, w2_ref, g_ref, b_ref, o_ref, *, alpha):
    x = x_ref[...]
    h = _dot(x.astype(BF16), w1_ref[...])
    act = (jax.nn.silu(h[:, :D_FF]) * h[:, D_FF:]).astype(BF16)
    y = _dot(act, w2_ref[...])
    o_ref[...] = _ln(alpha * x + y, g_ref[...], b_ref[...])


def _ffn_dense(x, w1, w2, g, b, *, alpha, tm, name):
    rows = x.shape[0]
    row = lambda n: pl.BlockSpec((tm, n), lambda i: (i, 0))
    return pl.pallas_call(
        functools.partial(_ffn_kernel, alpha=alpha),
        grid=(rows // tm,),
        in_specs=[row(D_MODEL), _full((D_MODEL, 2 * D_FF)), _full((D_FF, D_MODEL)),
                  _full((1, D_MODEL)), _full((1, D_MODEL))],
        out_specs=row(D_MODEL),
        out_shape=jax.ShapeDtypeStruct((rows, D_MODEL), F32),
        compiler_params=_cparams(("parallel",)),
        name=name,
    )(x, w1, w2, g, b)


def _finish_compress(ac, w2):
    n = ac.shape[0]
    a, c = ac[:, :KV_COLS], ac[:, KV_COLS:]
    row = lax.broadcasted_iota(jnp.int32, (n, KV_COLS), 0)
    c_next = jnp.where(row == n - 1, 0.0, pltpu.roll(c, n - 1, 0))
    h = jax.nn.gelu(a + c_next)
    return _dot(h.astype(BF16), w2)


def _compress_prompt_kernel(xs_ref, w1_ref, w2_ref, o_ref):
    o_ref[...] = _finish_compress(_dot(xs_ref[...], w1_ref[...]), w2_ref[...])


def _compress_prompt(kvc_rows, w):
    t = kvc_rows.shape[0]
    n_sub = t // CMP_STRIDE
    xs = kvc_rows.reshape(n_sub, CMP_STRIDE * KV_COLS)
    return pl.pallas_call(
        _compress_prompt_kernel,
        grid=(1,),
        in_specs=[_full(xs.shape), _full(w["phi1_rows"].shape), _full(w["phi2_big"].shape)],
        out_specs=_full((n_sub, KV_COLS)),
        out_shape=jax.ShapeDtypeStruct((n_sub, KV_COLS), F32),
        compiler_params=_cparams(("arbitrary",)),
        name="compress_prompt",
    )(xs, w["phi1_rows"], w["phi2_big"])


NEG_INF = float("-inf")


def _slope(head):
    return 2.0 ** -(head + 1)


def _cmp_probs(s, dist, valid, slope):
    s = jnp.where(valid, s - slope * dist, NEG_INF)
    m = jnp.max(s, axis=-1, keepdims=True)
    m = jnp.where(jnp.isfinite(m), m, 0.0)
    e = jnp.exp(s - m)
    return e / jnp.maximum(jnp.sum(e, axis=-1, keepdims=True), 1e-30)


def _select_blocks(pooled, cur, n_sel):
    blk = lax.broadcasted_iota(jnp.int32, pooled.shape, 1)
    blkf = blk.astype(F32)
    forced = (blk == 0) | (blk == cur) | (blk == cur - 1)
    work = jnp.where(blk <= cur, jnp.where(forced, jnp.inf, pooled), NEG_INF)
    sel = jnp.zeros(pooled.shape, F32)
    picks = []
    for _ in range(min(N_SEL_BLOCKS, n_sel)):
        m = jnp.max(work, axis=-1, keepdims=True)
        idx = jnp.min(jnp.where(work == m, blkf, 1e9), axis=-1, keepdims=True)
        pick = blkf == idx
        sel = jnp.maximum(sel, jnp.where(pick, jnp.where(m > NEG_INF, 1.0, 0.0), 0.0))
        work = jnp.where(pick, NEG_INF, work)
        picks.append(idx)
    return sel, picks


def _cmp_select_kernel(q_ref, kbd_ref, vl_ref, vr_ref, cpos_ref, oc_ref, unsel_ref, flag_ref, *,
                       n_sel, tq):
    n_cmp = 4 * n_sel
    q0 = pl.program_id(0) * tq
    qpos = q0 + lax.broadcasted_iota(jnp.int32, (tq, n_cmp), 0)
    dist = qpos.astype(F32) - cpos_ref[...]
    valid = dist >= 0
    cur = lax.shift_right_logical(q0 + lax.broadcasted_iota(jnp.int32, (tq, n_sel), 0), 6)
    for kvh in range(N_KV_HEADS):
        imp = jnp.zeros((tq, n_cmp), F32)
        for p in range(GROUP // 2):
            cols = slice((kvh * 2 + p) * LANES, (kvh * 2 + p + 1) * LANES)
            s2 = _dot(q_ref[:, cols], kbd_ref[kvh])
            o_pair = jnp.zeros((tq, LANES), F32)
            for j, v_ref in enumerate((vl_ref, vr_ref)):
                head = kvh * GROUP + 2 * p + j
                pr = _cmp_probs(s2[:, j * n_cmp:(j + 1) * n_cmp], dist, valid, _slope(head))
                imp = imp + pr
                o_pair = o_pair + _dot(pr.astype(BF16), v_ref[kvh])
            oc_ref[:, cols] = o_pair
        pooled = imp[:, :n_sel]
        for r in range(1, 4):
            pooled = pooled + imp[:, r * n_sel:(r + 1) * n_sel]
        sel, _ = _select_blocks(pooled, cur, n_sel)
        unsel_ref[kvh] = (1.0 - sel).astype(unsel_ref.dtype)
        flag_ref[0, kvh:kvh + 1, :] = jnp.max(sel, axis=0, keepdims=True).astype(jnp.int32)


def _cmp_select(q, cmp_kv, *, tq):
    t = q.shape[0]
    n_cmp = cmp_kv.shape[0]
    n_sel = n_cmp // 4
    perm = cmp_kv.reshape(n_sel, 4, N_KV_HEADS, 2, HEAD_DIM).transpose(2, 3, 1, 0, 4)
    perm = perm.reshape(N_KV_HEADS, 2, n_cmp, HEAD_DIM)
    kc, vc = perm[:, 0], perm[:, 1]
    kct = jnp.swapaxes(kc, 1, 2).astype(BF16)
    z = jnp.zeros_like(kct)
    kbd = jnp.concatenate([jnp.concatenate([kct, z], axis=2), jnp.concatenate([z, kct], axis=2)], axis=1)
    zv = jnp.zeros_like(vc)
    vl = jnp.concatenate([vc, zv], axis=2).astype(BF16)
    vr = jnp.concatenate([zv, vc], axis=2).astype(BF16)
    col = jnp.arange(n_cmp)
    cpos = ((4 * (col % n_sel) + col // n_sel) * CMP_STRIDE + (CMP_LEN - 1)).astype(F32)[None, :]
    row = lambda n: pl.BlockSpec((tq, n), lambda i: (i, 0))
    return pl.pallas_call(
        functools.partial(_cmp_select_kernel, n_sel=n_sel, tq=tq),
        grid=(t // tq,),
        in_specs=[row(NSA_WIDTH), _full(kbd.shape), _full(vl.shape), _full(vr.shape), _full(cpos.shape)],
        out_specs=(row(NSA_WIDTH), pl.BlockSpec((N_KV_HEADS, tq, n_sel), lambda i: (0, i, 0)),
                   pl.BlockSpec((1, N_KV_HEADS, n_sel), lambda i: (i, 0, 0))),
        out_shape=(jax.ShapeDtypeStruct((t, NSA_WIDTH), F32),
                   jax.ShapeDtypeStruct((N_KV_HEADS, t, n_sel), BF16),
                   jax.ShapeDtypeStruct((t // tq, N_KV_HEADS, n_sel), jnp.int32)),
        compiler_params=_cparams(("parallel",)),
        name="cmp_select_prompt",
    )(q, kbd, vl, vr, cpos)


MASKED = -1e30


def _attend_chunk(q4, kt, vt, dist, valid, kvh, bd_ref, va_ref, m_ref, acc_ref):
    for h in range(GROUP):
        bd_ref[h * HEAD_DIM:(h + 1) * HEAD_DIM, h * LANES:(h + 1) * LANES] = kt
    va_ref[0, :HEAD_DIM, :] = vt
    va_ref[1, HEAD_DIM:, :] = vt
    s4 = _dot(q4, bd_ref[...])
    for h in range(GROUP):
        s = s4[:, h * LANES:(h + 1) * LANES] - _slope(kvh * GROUP + h) * dist
        s = jnp.where(valid, s, MASKED)
        m_old = m_ref[h]
        m_new = jnp.maximum(m_old, jnp.max(s, axis=-1, keepdims=True))
        p = jnp.where(valid, jnp.exp(s - m_new), 0.0)
        acc_ref[h] = jnp.exp(m_old - m_new) * acc_ref[h] + _dot_nt(p.astype(BF16), va_ref[h % 2])
        m_ref[h] = m_new


def _attend_init(bd_ref, va_ref, m_ref, acc_ref):
    bd_ref[...] = jnp.zeros(bd_ref.shape, bd_ref.dtype)
    va_ref[...] = jnp.ones(va_ref.shape, va_ref.dtype)
    m_ref[...] = jnp.full(m_ref.shape, MASKED, F32)
    acc_ref[...] = jnp.zeros(acc_ref.shape, F32)


def _attend_result(acc_ref, pair):
    even, odd = acc_ref[2 * pair], acc_ref[2 * pair + 1]
    lane = lax.broadcasted_iota(jnp.int32, even.shape, 1)
    l_even = jnp.maximum(even[:, HEAD_DIM:HEAD_DIM + 1], 1e-30)
    l_odd = jnp.maximum(odd[:, 0:1], 1e-30)
    return jnp.where(lane < HEAD_DIM, even / l_even, odd / l_odd)


def _expand_gates(gates, branch, shape):
    lane_head = lax.broadcasted_iota(jnp.int32, shape, 1) // HEAD_DIM
    g = jnp.zeros(shape, F32)
    for h in range(N_HEADS):
        c = branch * N_HEADS + h
        g = jnp.where(lane_head == h, gates[:, c:c + 1], g)
    return g


def _slc_win_kernel(flag_ref, q_ref, kv_ref, unsel_ref, gates_ref, oc_ref, o_ref,
                    bd_ref, va_ref, m_ref, acc_ref, out_ref, *, n_sel, tq):
    i = pl.program_id(0)
    r_minus_lane = (lax.broadcasted_iota(jnp.int32, (tq, LANES), 0)
                    - lax.broadcasted_iota(jnp.int32, (tq, LANES), 1)).astype(F32)
    blk_row = lax.broadcasted_iota(jnp.int32, (n_sel, LANES), 0)
    half = lax.shift_right_logical(lax.broadcasted_iota(jnp.int32, (n_sel, LANES), 1), 6)
    gates = gates_ref[...]
    out_ref[...] = _expand_gates(gates, 0, (tq, NSA_WIDTH)) * oc_ref[...]

    for kvh in range(N_KV_HEADS):
        q4 = q_ref[:, kvh * GROUP * HEAD_DIM:(kvh + 1) * GROUP * HEAD_DIM]
        for branch in (1, 2):
            row0 = (branch - 1) * KV_COLS + kvh * 2 * HEAD_DIM
            _attend_init(bd_ref, va_ref, m_ref, acc_ref)

            def chunk(c, carry, branch=branch, row0=row0, kvh=kvh, q4=q4):
                lanes = pl.ds(pl.multiple_of(c * LANES, LANES), LANES)
                dist = r_minus_lane + ((i - c) * LANES).astype(F32)

                def run(valid):
                    _attend_chunk(q4, kv_ref[row0:row0 + HEAD_DIM, lanes],
                                  kv_ref[row0 + HEAD_DIM:row0 + 2 * HEAD_DIM, lanes],
                                  dist, valid, kvh, bd_ref, va_ref, m_ref, acc_ref)

                if branch == 1:
                    @pl.when((flag_ref[0, kvh, 2 * c] + flag_ref[0, kvh, 2 * c + 1]) > 0)
                    def _():
                        expand = jnp.where(blk_row == 2 * c + half, 1.0, 0.0).astype(BF16)
                        unsel = _dot(unsel_ref[kvh], expand)
                        run((unsel < 0.5) & (dist >= 0))
                else:
                    run((dist >= 0) & (dist <= WINDOW))
                return carry

            if branch == 1:
                lax.fori_loop(0, i + 1, chunk, 0)
            else:
                lax.fori_loop(jnp.maximum(i - WINDOW // LANES, 0), i + 1, chunk, 0)
            g = _expand_gates(gates, branch, (tq, NSA_WIDTH))
            for pair in range(GROUP // 2):
                cols = slice((kvh * 2 + pair) * LANES, (kvh * 2 + pair + 1) * LANES)
                out_ref[:, cols] += g[:, cols] * _attend_result(acc_ref, pair)
    o_ref[...] = out_ref[...].astype(o_ref.dtype)


def _slc_win(q, kvtb, unsel, flags, gates, o_cmp, *, tq):
    t = q.shape[0]
    n_sel = unsel.shape[-1]
    row = lambda n: pl.BlockSpec((tq, n), lambda i: (i, 0))
    return pl.pallas_call(
        functools.partial(_slc_win_kernel, n_sel=n_sel, tq=tq),
        grid=(t // tq,),
        in_specs=[pl.BlockSpec((1, N_KV_HEADS, n_sel), lambda i: (i, 0, 0), memory_space=pltpu.SMEM),
                  row(NSA_WIDTH), _full(kvtb.shape),
                  pl.BlockSpec((N_KV_HEADS, tq, n_sel), lambda i: (0, i, 0)), row(LANES), row(NSA_WIDTH)],
        out_specs=row(NSA_WIDTH),
        out_shape=jax.ShapeDtypeStruct((t, NSA_WIDTH), BF16),
        scratch_shapes=[pltpu.VMEM((GROUP * HEAD_DIM, GROUP * LANES), BF16),
                        pltpu.VMEM((2, LANES, LANES), BF16),
                        pltpu.VMEM((GROUP, tq, LANES), F32),
                        pltpu.VMEM((GROUP, tq, LANES), F32),
                        pltpu.VMEM((tq, NSA_WIDTH), F32)],
        compiler_params=_cparams(("parallel",)),
        name="slc_win_prompt",
    )(flags, q, kvtb, unsel, gates, o_cmp)


PAGES_PER_STEP = 16


def _compress_sample_kernel(pt_ref, *refs, n_pages):
    page_refs, (w1_ref, ac_ref, xs_ref) = refs[:n_pages], refs[n_pages:]
    for p in range(n_pages):
        for kvh in range(N_KV_HEADS):
            page = page_refs[p][0, 0, kvh].reshape(LANES, PAGE_SIZE)
            xs_ref[kvh, p * PAGE_SIZE:(p + 1) * PAGE_SIZE, :] = page.T
    n = n_pages * PAGE_SIZE // CMP_STRIDE
    acc = jnp.zeros((n, 2 * KV_COLS), F32)
    for l in range(CMP_STRIDE):
        rows_l = jnp.concatenate([xs_ref[kvh, pl.ds(l, n, stride=CMP_STRIDE), :]
                                  for kvh in range(N_KV_HEADS)], axis=1)
        acc = acc + _dot(rows_l.astype(BF16), w1_ref[l])
    ac_ref[0] = acc


def _compress_sample(cache_t, page_table, w, layer):
    b, n_pg = page_table.shape
    p = PAGES_PER_STEP
    n = p * PAGE_SIZE // CMP_STRIDE
    w1 = w["phi1_rows"].reshape(CMP_STRIDE, KV_COLS, 2 * KV_COLS)

    def page_spec(j):
        return pl.BlockSpec((1, 1, N_KV_HEADS, 2, HEAD_DIM, PAGE_SIZE),
                            lambda bi, c, pt: (layer, pt[bi, c * p + j], 0, 0, 0, 0))

    return pl.pallas_call(
        functools.partial(_compress_sample_kernel, n_pages=p),
        grid_spec=pltpu.PrefetchScalarGridSpec(
            num_scalar_prefetch=1,
            grid=(b, n_pg // p),
            in_specs=[page_spec(j) for j in range(p)]
            + [pl.BlockSpec(w1.shape, lambda bi, c, pt: (0, 0, 0))],
            out_specs=pl.BlockSpec((1, n, 2 * KV_COLS), lambda bi, c, pt: (bi, c, 0)),
            scratch_shapes=[pltpu.VMEM((N_KV_HEADS, p * PAGE_SIZE, LANES), F32)],
        ),
        out_shape=jax.ShapeDtypeStruct((b, n_pg * PAGE_SIZE // CMP_STRIDE, 2 * KV_COLS), F32),
        compiler_params=_cparams(("parallel", "parallel")),
        name="compress_sample",
    )(page_table, *([cache_t] * p), w1)


def _finish_sample_kernel(ac_ref, w2_ref, kvt_ref, res_ref):
    res = _finish_compress(ac_ref[0], w2_ref[...])
    n_sel = res.shape[0] // 4
    for kvh in range(N_KV_HEADS):
        res_ref[kvh] = res[:, kvh * LANES:(kvh + 1) * LANES]
        for r in range(4):
            kv = res_ref[kvh, pl.ds(r, n_sel, stride=4), :]
            kvt_ref[0, kvh, :, r * n_sel:(r + 1) * n_sel] = kv.T.astype(kvt_ref.dtype)


def _finish_sample(ac, w):
    b, n_cmp, _ = ac.shape
    return pl.pallas_call(
        _finish_sample_kernel,
        grid=(b,),
        in_specs=[pl.BlockSpec((1, n_cmp, 2 * KV_COLS), lambda i: (i, 0, 0)), _full((KV_COLS, KV_COLS))],
        out_specs=pl.BlockSpec((1, N_KV_HEADS, LANES, n_cmp), lambda i: (i, 0, 0, 0)),
        out_shape=jax.ShapeDtypeStruct((b, N_KV_HEADS, LANES, n_cmp), BF16),
        scratch_shapes=[pltpu.VMEM((N_KV_HEADS, n_cmp, LANES), F32)],
        compiler_params=_cparams(("parallel",)),
        name="finish_compress_sample",
    )(ac, w["phi2_big"])


def _cmp_select_sample_kernel(q_ref, kvt_ref, cpos_ref, slope_ref, oc_ref, idx_ref, *, n_sel, t_past):
    n_cmp = 4 * n_sel
    q = q_ref[0].astype(BF16)
    group0 = lax.broadcasted_iota(jnp.int32, (N_HEADS, n_cmp), 0) < GROUP
    s = jnp.where(group0, _dot(q, kvt_ref[0, 0, :HEAD_DIM, :]), _dot(q, kvt_ref[0, 1, :HEAD_DIM, :]))
    dist = jnp.broadcast_to(float(t_past) - cpos_ref[...], (N_HEADS, n_cmp))
    pr = _cmp_probs(s, dist, dist >= 0, slope_ref[:, 0:1])
    prb = pr.astype(BF16)
    oc_ref[0] = jnp.where(group0[:, :HEAD_DIM], _dot_nt(prb, kvt_ref[0, 0, HEAD_DIM:, :]),
                          _dot_nt(prb, kvt_ref[0, 1, HEAD_DIM:, :]))
    imp = jnp.where(group0, jnp.sum(pr[:GROUP], axis=0, keepdims=True),
                    jnp.sum(pr[GROUP:], axis=0, keepdims=True))
    pooled = imp[:, :n_sel]
    for r in range(1, 4):
        pooled = pooled + imp[:, r * n_sel:(r + 1) * n_sel]
    pooled = jnp.concatenate([pooled, jnp.zeros((N_HEADS, LANES), F32)], axis=1)
    cur = jnp.full(pooled.shape, t_past // SLC_BLOCK, jnp.int32)
    _, picks = _select_blocks(pooled, cur, pooled.shape[1])
    lane = lax.broadcasted_iota(jnp.int32, (N_HEADS, LANES), 1)
    idx = jnp.zeros((N_HEADS, LANES), F32)
    for it, pick in enumerate(picks):
        idx = jnp.where(lane == it, pick, idx)
    idx_ref[0] = idx.astype(jnp.int32)


def _cmp_select_sample(q8, kvt_cmp, slopes, *, t_past):
    b = q8.shape[0]
    n_cmp = kvt_cmp.shape[-1]
    n_sel = n_cmp // 4
    col = jnp.arange(n_cmp)
    cpos = ((4 * (col % n_sel) + col // n_sel) * CMP_STRIDE + (CMP_LEN - 1)).astype(F32)[None, :]
    return pl.pallas_call(
        functools.partial(_cmp_select_sample_kernel, n_sel=n_sel, t_past=t_past),
        grid=(b,),
        in_specs=[pl.BlockSpec((1, N_HEADS, HEAD_DIM), lambda i: (i, 0, 0)),
                  pl.BlockSpec((1, N_KV_HEADS, LANES, n_cmp), lambda i: (i, 0, 0, 0)),
                  _full(cpos.shape), _full(slopes.shape)],
        out_specs=(pl.BlockSpec((1, N_HEADS, HEAD_DIM), lambda i: (i, 0, 0)),
                   pl.BlockSpec((1, N_HEADS, LANES), lambda i: (i, 0, 0))),
        out_shape=(jax.ShapeDtypeStruct((b, N_HEADS, HEAD_DIM), F32),
                   jax.ShapeDtypeStruct((b, N_HEADS, LANES), jnp.int32)),
        compiler_params=_cparams(("parallel",)),
        name="cmp_select_sample",
    )(q8, kvt_cmp, cpos, slopes)


def _softmax_step(state, s, valid, vt):
    m, l, acc = state
    s = jnp.where(valid, s, MASKED)
    m_new = jnp.maximum(m, jnp.max(s, axis=-1, keepdims=True))
    p = jnp.where(valid, jnp.exp(s - m_new), 0.0)
    a = jnp.exp(m - m_new)
    return m_new, a * l + jnp.sum(p, axis=-1, keepdims=True), a * acc + _dot_nt(p.astype(BF16), vt)


def _softmax_new_row(state, q, k_rows, v_rows):
    m, l, acc = state
    s = jnp.sum(q * k_rows, axis=-1, keepdims=True)
    m_new = jnp.maximum(m, s)
    p = jnp.exp(s - m_new)
    a = jnp.exp(m - m_new)
    l, acc = a * l + p, a * acc + p * v_rows
    return acc / jnp.maximum(l, 1e-30)


def _slc_win_sample_kernel(sel_ref, pt_ref, q_ref, *refs, n_top, t_past):
    n_pages = N_KV_HEADS * n_top
    page_refs = refs[:n_pages]
    win_ref, snew_ref, wnew_ref, gates_ref, oc_ref, slope_ref, o_ref = refs[n_pages:]
    b = pl.program_id(0)
    q = q_ref[0]
    qb = q.astype(BF16)
    slope = slope_ref[:, 0:1]
    row_kvh = lax.shift_right_logical(lax.broadcasted_iota(jnp.int32, (N_HEADS, LANES), 0), 2)
    lane = lax.broadcasted_iota(jnp.int32, (N_HEADS, LANES), 1)
    lane_half = lax.shift_right_logical(lane, 6)
    init = (jnp.full((N_HEADS, 1), MASKED, F32), jnp.zeros((N_HEADS, 1), F32),
            jnp.zeros((N_HEADS, HEAD_DIM), F32))

    state = init
    for kvh in range(N_KV_HEADS):
        for j in range(n_top):
            blk = sel_ref[b, kvh * n_top + j]
            page = page_refs[kvh * n_top + j]
            kpos = lax.shift_right_logical(blk, 1) * PAGE_SIZE + lane
            dist = (t_past - kpos).astype(F32)
            valid = (lane_half == (blk & 1)) & (row_kvh == kvh) & (blk < t_past // SLC_BLOCK)
            s = _dot(qb, page[0, 0, 0, 0].astype(BF16)) - slope * dist
            state = _softmax_step(state, s, valid, page[0, 0, 0, 1].astype(BF16))
    o_slc = _softmax_new_row(state, q, snew_ref[0, 0], snew_ref[0, 1])

    state = init
    w = win_ref.shape[-1]
    wlane = lax.broadcasted_iota(jnp.int32, (N_HEADS, w), 1)
    wdist = (w - wlane).astype(F32)
    wrow_kvh = lax.shift_right_logical(lax.broadcasted_iota(jnp.int32, (N_HEADS, w), 0), 2)
    for kvh in range(N_KV_HEADS):
        s = _dot(qb, win_ref[0, 0, kvh, 0].astype(BF16)) - slope * wdist
        valid = (wrow_kvh == kvh) & (wdist <= WINDOW)
        state = _softmax_step(state, s, valid, win_ref[0, 0, kvh, 1].astype(BF16))
    o_win = _softmax_new_row(state, q, wnew_ref[0, 0], wnew_ref[0, 1])

    o_ref[0] = gates_ref[0, 0] * oc_ref[0] + gates_ref[0, 1] * o_slc + gates_ref[0, 2] * o_win


def _slc_win_sample(sel, page_table, q8, slc_t, win_t, slc_new, win_new, gates, o_cmp, slopes, layer, *,
                    t_past):
    b, n_pg = page_table.shape
    n_top = sel.shape[1] // N_KV_HEADS
    w = win_t.shape[-1]

    def page_spec(kvh, j):
        def index(bi, sel_ref, pt_ref):
            pg = jnp.minimum(lax.shift_right_logical(sel_ref[bi, kvh * n_top + j], 1), n_pg - 1)
            return (layer, pt_ref[bi, pg], kvh, 0, 0, 0)
        return pl.BlockSpec((1, 1, 1, 2, HEAD_DIM, PAGE_SIZE), index)

    per_seq = lambda *tail: pl.BlockSpec((1,) + tail, lambda bi, s, p: (bi,) + (0,) * len(tail))
    return pl.pallas_call(
        functools.partial(_slc_win_sample_kernel, n_top=n_top, t_past=t_past),
        grid_spec=pltpu.PrefetchScalarGridSpec(
            num_scalar_prefetch=2,
            grid=(b,),
            in_specs=[per_seq(N_HEADS, HEAD_DIM)]
            + [page_spec(kvh, j) for kvh in range(N_KV_HEADS) for j in range(n_top)]
            + [pl.BlockSpec((1, 1, N_KV_HEADS, 2, HEAD_DIM, w), lambda bi, s, p: (layer, bi, 0, 0, 0, 0)),
               per_seq(2, N_HEADS, HEAD_DIM), per_seq(2, N_HEADS, HEAD_DIM), per_seq(3, N_HEADS, 1),
               per_seq(N_HEADS, HEAD_DIM), pl.BlockSpec(slopes.shape, lambda bi, s, p: (0, 0))],
            out_specs=per_seq(N_HEADS, HEAD_DIM),
        ),
        out_shape=jax.ShapeDtypeStruct((b, N_HEADS, HEAD_DIM), F32),
        compiler_params=_cparams(("parallel",)),
        name="slc_win_sample",
    )(sel, page_table, q8, *([slc_t] * (N_KV_HEADS * n_top)), win_t, slc_new, win_new, gates, o_cmp, slopes)


MOE_TOKENS = 256
MOE_SLOTS = 256


def _n_slot_blocks(n_tokens):
    return -(-(2 * n_tokens + N_EXPERTS * (MOE_SLOTS - 1)) // MOE_SLOTS)


def _route_kernel(x_ref, wr_ref, dest_ref, wts_ref, blk_ref, count_ref, base_ref, *, n_blocks):
    phase, i = pl.program_id(0), pl.program_id(1)
    tt = x_ref.shape[0]
    logits = lax.dot_general(wr_ref[...], x_ref[...], (((1,), (1,)), ((), ())),
                             precision=lax.Precision.HIGHEST, preferred_element_type=F32)
    row = lax.broadcasted_iota(jnp.int32, (N_EXPERTS, tt), 0)
    rowf = row.astype(F32)
    m1 = jnp.max(logits, axis=0, keepdims=True)
    e1 = jnp.min(jnp.where(logits == m1, rowf, float(N_EXPERTS)), axis=0, keepdims=True)
    rest = jnp.where(rowf == e1, NEG_INF, logits)
    m2 = jnp.max(rest, axis=0, keepdims=True)
    e2 = jnp.min(jnp.where(rest == m2, rowf, float(N_EXPERTS)), axis=0, keepdims=True)
    hot1, hot2 = rowf == e1, rowf == e2
    onehot = jnp.where(hot1 | hot2, 1.0, 0.0)
    tile_count = jnp.sum(onehot, axis=1, keepdims=True)

    @pl.when((phase == 0) & (i == 0))
    def _():
        count_ref[...] = jnp.zeros(count_ref.shape, F32)

    @pl.when(phase == 0)
    def _():
        count_ref[...] += tile_count

    @pl.when((phase == 1) & (i == 0))
    def _():
        padded = jnp.ceil(count_ref[...] / MOE_SLOTS) * MOE_SLOTS
        erow = lax.broadcasted_iota(jnp.int32, padded.shape, 0)
        start = jnp.zeros(padded.shape, F32)
        end = jnp.zeros(padded.shape, F32)
        for e in range(N_EXPERTS):
            pe = padded[e:e + 1, :]
            start = start + jnp.where(erow > e, pe, 0.0)
            end = end + jnp.where(erow >= e, pe, 0.0)
        base_ref[...] = start
        first_row = lax.broadcasted_iota(jnp.int32, (N_EXPERTS, blk_ref.shape[1]), 1).astype(F32) * MOE_SLOTS
        owner = jnp.sum(jnp.where(end[:, 0:1] <= first_row, 1.0, 0.0), axis=0, keepdims=True)
        used = jnp.max(end[:, 0:1], axis=0, keepdims=True) / MOE_SLOTS
        r8 = lax.broadcasted_iota(jnp.int32, blk_ref.shape, 0)
        blk = jnp.where(r8 == 0, jnp.minimum(owner, N_EXPERTS - 1.0), jnp.broadcast_to(used, blk_ref.shape))
        blk_ref[...] = blk.astype(jnp.int32)

    @pl.when(phase == 1)
    def _():
        upper = (lax.broadcasted_iota(jnp.int32, (tt, tt), 0)
                 < lax.broadcasted_iota(jnp.int32, (tt, tt), 1))
        before = _dot(onehot.astype(BF16), jnp.where(upper, 1.0, 0.0).astype(BF16))
        slot = base_ref[:, 0:1] + before
        d1 = jnp.sum(jnp.where(hot1, slot, 0.0), axis=0, keepdims=True)
        d2 = jnp.sum(jnp.where(hot2, slot, 0.0), axis=0, keepdims=True)
        r8 = lax.broadcasted_iota(jnp.int32, (8, tt), 0)
        dest_ref[0] = jnp.where(r8 == 0, d1, d2).astype(jnp.int32)
        z = jnp.exp(m2 - m1)
        w1 = 1.0 / (1.0 + z)
        wts_ref[0] = jnp.where(r8 == 0, w1, z * w1)
        base_ref[...] += tile_count


def _route(x, w_router_t):
    n = x.shape[0]
    nt = n // MOE_TOKENS
    n_blocks = _n_slot_blocks(n)
    blk_lanes = -(-n_blocks // LANES) * LANES
    tile = lambda ph, i: (i * ph, 0, 0)
    return pl.pallas_call(
        functools.partial(_route_kernel, n_blocks=n_blocks),
        grid=(2, nt),
        in_specs=[pl.BlockSpec((MOE_TOKENS, D_MODEL), lambda ph, i: (i, 0)),
                  pl.BlockSpec(w_router_t.shape, lambda ph, i: (0, 0))],
        out_specs=(pl.BlockSpec((1, 8, MOE_TOKENS), tile), pl.BlockSpec((1, 8, MOE_TOKENS), tile),
                   pl.BlockSpec((8, blk_lanes), lambda ph, i: (0, 0))),
        out_shape=(jax.ShapeDtypeStruct((nt, 8, MOE_TOKENS), jnp.int32),
                   jax.ShapeDtypeStruct((nt, 8, MOE_TOKENS), F32),
                   jax.ShapeDtypeStruct((8, blk_lanes), jnp.int32)),
        scratch_shapes=[pltpu.VMEM((N_EXPERTS, LANES), F32), pltpu.VMEM((N_EXPERTS, LANES), F32)],
        compiler_params=_cparams(("arbitrary", "arbitrary")),
        name="moe_route",
    )(x, w_router_t)


def _row_copy(src, src_row, dst, dst_row, sem):
    return pltpu.make_async_copy(src.at[pl.ds(src_row, 1), :], dst.at[pl.ds(dst_row, 1), :], sem)


def _scatter_kernel(d1_ref, d2_ref, x_ref, init_ref, xs_ref, sem):
    del init_ref
    i = pl.program_id(0)
    tt = x_ref.shape[0]

    def issue(r, carry):
        t = i * tt + r
        _row_copy(x_ref, r, xs_ref, d1_ref[t], sem).start()
        _row_copy(x_ref, r, xs_ref, d2_ref[t], sem).start()
        return carry

    lax.fori_loop(0, tt, issue, 0)

    def drain(r, carry):
        _row_copy(x_ref, 0, xs_ref, 0, sem).wait()
        return carry

    lax.fori_loop(0, 2 * tt, drain, 0)


def _scatter_rows(x, d1, d2, n_slots):
    n = x.shape[0]
    init = jnp.zeros((n_slots, D_MODEL), F32)
    return pl.pallas_call(
        _scatter_kernel,
        grid_spec=pltpu.PrefetchScalarGridSpec(
            num_scalar_prefetch=2,
            grid=(n // MOE_TOKENS,),
            in_specs=[pl.BlockSpec((MOE_TOKENS, D_MODEL), lambda i, a, b: (i, 0)),
                      pl.BlockSpec(memory_space=pl.ANY)],
            out_specs=pl.BlockSpec(memory_space=pl.ANY),
            scratch_shapes=[pltpu.SemaphoreType.DMA(())],
        ),
        out_shape=jax.ShapeDtypeStruct((n_slots, D_MODEL), F32),
        input_output_aliases={3: 0},
        compiler_params=_cparams(("arbitrary",)),
        name="moe_scatter",
    )(d1, d2, x, init)


FF_HALF = D_FF // 2


def _expert_ffn_kernel(blk_ref, x_ref, w1_ref, w2_ref, y_ref):
    j = pl.program_id(0)

    @pl.when(j < blk_ref[1, 0])
    def _():
        x = x_ref[...].astype(BF16)
        y = jnp.zeros(y_ref.shape, F32)
        for c in range(2):
            g = _dot(x, w1_ref[0, :, c * FF_HALF:(c + 1) * FF_HALF])
            u = _dot(x, w1_ref[0, :, D_FF + c * FF_HALF:D_FF + (c + 1) * FF_HALF])
            act = (jax.nn.silu(g) * u).astype(BF16)
            y = y + _dot(act, w2_ref[0, c * FF_HALF:(c + 1) * FF_HALF, :])
        y_ref[...] = y

    @pl.when(j >= blk_ref[1, 0])
    def _():
        y_ref[...] = jnp.zeros(y_ref.shape, F32)


def _expert_ffn(xs, blk, w1, w2):
    n_slots = xs.shape[0]
    return pl.pallas_call(
        _expert_ffn_kernel,
        grid_spec=pltpu.PrefetchScalarGridSpec(
            num_scalar_prefetch=1,
            grid=(n_slots // MOE_SLOTS,),
            in_specs=[pl.BlockSpec((MOE_SLOTS, D_MODEL), lambda j, b: (j, 0)),
                      pl.BlockSpec((1, D_MODEL, 2 * D_FF), lambda j, b: (b[0, j], 0, 0)),
                      pl.BlockSpec((1, D_FF, D_MODEL), lambda j, b: (b[0, j], 0, 0))],
            out_specs=pl.BlockSpec((MOE_SLOTS, D_MODEL), lambda j, b: (j, 0)),
        ),
        out_shape=jax.ShapeDtypeStruct((n_slots, D_MODEL), F32),
        compiler_params=_cparams(("arbitrary",)),
        name="moe_expert_ffn",
    )(blk, xs, w1, w2)


def _combine_kernel(d1_ref, d2_ref, ys_ref, x_ref, wts_ref, g_ref, b_ref, o_ref, buf_ref, sem, *, alpha):
    i = pl.program_id(0)
    tt = x_ref.shape[0]

    def issue(r, carry):
        t = i * tt + r
        _row_copy(ys_ref, d1_ref[t], buf_ref.at[0], r, sem).start()
        _row_copy(ys_ref, d2_ref[t], buf_ref.at[1], r, sem).start()
        return carry

    lax.fori_loop(0, tt, issue, 0)

    def drain(r, carry):
        _row_copy(ys_ref, 0, buf_ref.at[0], 0, sem).wait()
        return carry

    lax.fori_loop(0, 2 * tt, drain, 0)
    wcol = wts_ref[0].T
    y = wcol[:, 0:1] * buf_ref[0] + wcol[:, 1:2] * buf_ref[1]
    o_ref[...] = _ln(alpha * x_ref[...] + y, g_ref[...], b_ref[...])


def _combine(ys, x, d1, d2, wts, g, b, *, alpha):
    n = x.shape[0]
    row = pl.BlockSpec((MOE_TOKENS, D_MODEL), lambda i, p, q: (i, 0))
    vec = pl.BlockSpec((1, D_MODEL), lambda i, p, q: (0, 0))
    return pl.pallas_call(
        functools.partial(_combine_kernel, alpha=alpha),
        grid_spec=pltpu.PrefetchScalarGridSpec(
            num_scalar_prefetch=2,
            grid=(n // MOE_TOKENS,),
            in_specs=[pl.BlockSpec(memory_space=pl.ANY), row,
                      pl.BlockSpec((1, 8, MOE_TOKENS), lambda i, p, q: (i, 0, 0)), vec, vec],
            out_specs=row,
            scratch_shapes=[pltpu.VMEM((2, MOE_TOKENS, D_MODEL), F32), pltpu.SemaphoreType.DMA(())],
        ),
        out_shape=jax.ShapeDtypeStruct((n, D_MODEL), F32),
        compiler_params=_cparams(("arbitrary",)),
        name="moe_combine",
    )(d1, d2, ys, x, wts, g, b)


def _moe_ffn(x, w_router_t, w1, w2, g, b, *, alpha):
    n = x.shape[0]
    dest, wts, blk = _route(x, w_router_t)
    d1, d2 = dest[:, 0, :].reshape(n), dest[:, 1, :].reshape(n)
    xs = _scatter_rows(x, d1, d2, _n_slot_blocks(n) * MOE_SLOTS)
    ys = _expert_ffn(xs, blk, w1, w2)
    return _combine(ys, x, d1, d2, wts, g, b, alpha=alpha)


def _prep_layer(l, w_in, w_phi1, w_phi2, gmlp_ln_g, gmlp_ln_b, gmlp_ws, gmlp_bs, w_out, ln1_g, ln1_b):
    wi = w_in[l]
    c0 = NSA_WIDTH
    c1, c2, c3 = c0 + KV_COLS, c0 + 2 * KV_COLS, c0 + 3 * KV_COLS
    c4 = c3 + N_GATES
    c5 = c4 + GMLP_WIDTH
    eye = jnp.eye(N_KV_HEADS, dtype=F32)
    w1 = w_phi1[l]
    wa = jnp.einsum("cldf,kK,cC->lkcdKCf", w1[:, :CMP_STRIDE], eye, eye)
    wc = jnp.einsum("cldf,kK,cC->lkcdKCf", w1[:, CMP_STRIDE:], eye, eye)
    rows = CMP_STRIDE * KV_COLS
    phi1_rows = jnp.concatenate([wa.reshape(rows, KV_COLS), wc.reshape(rows, KV_COLS)], axis=1)
    phi2_big = jnp.einsum("cfe,kK,cC->kcfKCe", w_phi2[l], eye, eye).reshape(KV_COLS, KV_COLS)
    ws, bs = gmlp_ws[l], gmlp_bs[l]
    g_bias = jnp.repeat(bs.reshape(GMLP_WIDTH // LANES, 2, CHUNK).transpose(0, 2, 1), GMLP_GROUP_DIM, axis=2)
    return {
        "wq": (wi[:, :c0] * SCALE).astype(BF16),
        "wkvt": wi[:, c0:c3].T.astype(BF16),
        "wkvc": wi[:, c0:c1].astype(BF16),
        "wg": jnp.pad(wi[:, c3:c4], ((0, 0), (0, LANES - N_GATES))).astype(BF16),
        "wu": wi[:, c4:c5].astype(BF16),
        "wv": wi[:, c5:].astype(BF16),
        "lng": gmlp_ln_g[l][None, :],
        "lnb": gmlp_ln_b[l][None, :],
        "g_tril": jnp.tril(ws).astype(BF16),
        "g_bias": g_bias,
        "g_diag_w": jnp.repeat(ws[:, 0, 0], GMLP_GROUP_DIM)[None, :],
        "g_diag_b": jnp.repeat(bs[:, 0], GMLP_GROUP_DIM)[None, :],
        "phi1_rows": phi1_rows.astype(BF16),
        "phi2_big": phi2_big.astype(BF16),
        "w_phi1": w1,
        "wo_a": w_out[l][:NSA_WIDTH].astype(BF16),
        "wo_g": w_out[l][NSA_WIDTH:].astype(BF16),
        "ln1_g": ln1_g[l][None, :],
        "ln1_b": ln1_b[l][None, :],
    }


def _kv_leaf(kvt):
    return kvt.reshape(N_KV_HEADS, 2, HEAD_DIM, kvt.shape[-1]).transpose(3, 0, 1, 2)


def _mix_prompt(xp, w, alpha):
    q, kvt, kvtb, kvc_rows, gates, og, _ = _inproj(xp, w, sample=False, tm=256)
    cmp_kv = _compress_prompt(kvc_rows, w)
    o_cmp, unsel, flags = _cmp_select(q, cmp_kv, tq=128)
    o_attn = _slc_win(q, kvtb, unsel, flags, gates, o_cmp, tq=128)
    x1 = _outproj(o_attn, og, xp, w, alpha=alpha, tm=256, name="outproj_prompt")
    return x1, kvt


def _cache_view(cache):
    return cache.transpose(0, 1, 3, 4, 5, 2)


def _mix_sample(xs, w, alpha, cmp_t, slc_t, win_t, page_table, layer):
    b = xs.shape[0]
    t_past = page_table.shape[1] * PAGE_SIZE
    q, kvt, _, _, gates, og, vn = _inproj(xs, w, sample=True, tm=b)
    kvt_cmp = _finish_sample(_compress_sample(cmp_t, page_table, w, layer), w)
    q8 = q.astype(F32).reshape(b, N_HEADS, HEAD_DIM)
    kv_new = kvt.T.reshape(b, 3, N_KV_HEADS, 2, HEAD_DIM)
    per_head = lambda kv: jnp.repeat(kv.transpose(0, 2, 1, 3), GROUP, axis=2)
    slopes = jnp.broadcast_to(jnp.exp2(-jnp.arange(1, N_HEADS + 1, dtype=F32))[:, None], (N_HEADS, LANES))
    o_cmp, idx = _cmp_select_sample(q8, kvt_cmp, slopes, t_past=t_past)
    sel = idx[:, ::GROUP, :N_SEL_BLOCKS].reshape(b, N_KV_HEADS * N_SEL_BLOCKS)
    gates8 = gates[:, :N_GATES].reshape(b, 3, N_HEADS, 1)
    o_attn = _slc_win_sample(sel, page_table, q8, slc_t, win_t, per_head(kv_new[:, 1]), per_head(kv_new[:, 2]),
                             gates8, o_cmp, slopes, layer, t_past=t_past)
    o_attn = o_attn.reshape(b, NSA_WIDTH).astype(BF16)
    x1 = _outproj(o_attn, og, xs, w, alpha=alpha, tm=b, name="outproj_sample")
    return x1, kv_new, vn


def kernel(x_prompt, x_sample, cache_cmp, cache_slc, cache_win, page_table, w_in, w_phi1, w_phi2, gmlp_ln_g,
           gmlp_ln_b, gmlp_ws, gmlp_bs, w_out, ln1_g, ln1_b, ln2_g, ln2_b, ffn_w1, ffn_w2, moe_router,
           moe_w1, moe_w2):
    depth = w_in.shape[0]
    alpha = (2 * depth) ** 0.25
    t, b = x_prompt.shape[1], x_sample.shape[0]
    xp, xs = x_prompt[0], x_sample[:, 0]
    cmp_t, slc_t, win_t = _cache_view(cache_cmp), _cache_view(cache_slc), _cache_view(cache_win)
    n_win = min(WINDOW, t)
    outs = [[] for _ in range(7)]
    for l in range(depth):
        w = _prep_layer(l, w_in, w_phi1, w_phi2, gmlp_ln_g, gmlp_ln_b, gmlp_ws, gmlp_bs, w_out, ln1_g, ln1_b)
        x1p, kvt = _mix_prompt(xp, w, alpha)
        x1s, kv_new, vn = _mix_sample(xs, w, alpha, cmp_t, slc_t, win_t, page_table, l)
        outs[0].append(_kv_leaf(kvt[:KV_COLS])[None])
        outs[1].append(_kv_leaf(kvt[KV_COLS:2 * KV_COLS])[None])
        outs[2].append(_kv_leaf(kvt[2 * KV_COLS:, t - n_win:])[None])
        for k in range(3):
            outs[3 + k].append(kv_new[:, k][:, None])
        outs[6].append(vn[:, None, :])
        g2, b2 = ln2_g[l][None, :], ln2_b[l][None, :]
        i = l // 2
        if l % 2 == 0:
            w1, w2 = ffn_w1[i].astype(BF16), ffn_w2[i].astype(BF16)
            xp = _ffn_dense(x1p, w1, w2, g2, b2, alpha=alpha, tm=256, name="ffn_prompt")
            xs = _ffn_dense(x1s, w1, w2, g2, b2, alpha=alpha, tm=b, name="ffn_sample")
        else:
            pad = -(t + b) % MOE_TOKENS
            x_all = jnp.concatenate([x1p, x1s, jnp.zeros((pad, D_MODEL), F32)], axis=0)
            y_all = _moe_ffn(x_all, moe_router[i].T, moe_w1[i].astype(BF16), moe_w2[i].astype(BF16), g2, b2,
                             alpha=alpha)
            xp, xs = y_all[:t], y_all[t:t + b]
    return (xp[None], xs[:, None]) + tuple(jnp.stack(o) for o in outs)
```

```python
import functools

import jax
import jax.numpy as jnp
import numpy as np
from jax import lax
from jax.experimental import pallas as pl
from jax.experimental.pallas import tpu as pltpu

F32 = jnp.float32
BF16 = jnp.bfloat16

D_MODEL = 1024
N_HEADS = 8
N_KV_HEADS = 2
GROUP = N_HEADS // N_KV_HEADS
HEAD_DIM = 64
NSA_WIDTH = N_HEADS * HEAD_DIM
SCALE = HEAD_DIM ** -0.5
CMP_STRIDE = 16
CMP_LEN = 32
SLC_BLOCK = 64
N_SEL_BLOCKS = 16
WINDOW = 512
GMLP_GROUPS = 8
GMLP_GROUP_DIM = 64
GMLP_WIDTH = GMLP_GROUPS * GMLP_GROUP_DIM
CHUNK = 128
KV_COLS = N_KV_HEADS * 2 * HEAD_DIM
N_GATES = 3 * N_HEADS
D_FF = 2816
N_EXPERTS = 8
PAGE_SIZE = 128
LN_EPS = 1e-5

LANES = 128
VMEM_LIMIT = 56 * 1024 * 1024


def _cparams(sem, vmem=VMEM_LIMIT):
    return pltpu.CompilerParams(dimension_semantics=sem, vmem_limit_bytes=vmem)


def _ln(x, g, b):
    xc = x - jnp.mean(x, axis=-1, keepdims=True)
    var = jnp.mean(xc * xc, axis=-1, keepdims=True)
    return xc * lax.rsqrt(var + LN_EPS) * g + b


def _dot(a, b):
    return jnp.dot(a, b, preferred_element_type=F32)


def _dot_nt(a, b):
    return lax.dot_general(a, b, (((1,), (1,)), ((), ())), preferred_element_type=F32)


def _full(shape):
    n = len(shape)
    return pl.BlockSpec(shape, lambda *_: (0,) * n)


def _inproj_kernel(x_ref, wq_ref, wkvt_ref, wkvc_ref, wg_ref, wu_ref, wv_ref, lng_ref, lnb_ref,
                   gw_ref, gb_ref, q_ref, kvt_ref, kvtb_ref, kvc_ref, gates_ref, og_ref, vn_ref, *,
                   sample):
    x = x_ref[...].astype(BF16)
    q_ref[...] = _dot(x, wq_ref[...]).astype(q_ref.dtype)
    kvt = _dot_nt(wkvt_ref[...], x)
    kvt_ref[...] = kvt
    kvtb_ref[...] = kvt[KV_COLS:].astype(kvtb_ref.dtype)
    kvc_ref[...] = _dot(x, wkvc_ref[...]).astype(kvc_ref.dtype)
    gates_ref[...] = jax.nn.sigmoid(_dot(x, wg_ref[...]))
    u = jax.nn.gelu(_dot(x, wu_ref[...]))
    v = jax.nn.gelu(_dot(x, wv_ref[...]))
    vn = _ln(v, lng_ref[...], lnb_ref[...])
    vn_ref[...] = vn
    if sample:
        og_ref[...] = (u * (vn * gw_ref[...] + gb_ref[...])).astype(og_ref.dtype)
    else:
        tm = x.shape[0]
        lane = lax.broadcasted_iota(jnp.int32, (CHUNK, LANES), 1)
        first = lane < GMLP_GROUP_DIM
        vb = vn.astype(BF16)
        for c in range(tm // CHUNK):
            rows = slice(c * CHUNK, (c + 1) * CHUNK)
            for p in range(GMLP_WIDTH // LANES):
                cols = slice(p * LANES, (p + 1) * LANES)
                vp = vb[rows, cols]
                mix = jnp.where(first, _dot(gw_ref[2 * p], vp), _dot(gw_ref[2 * p + 1], vp))
                og_ref[rows, cols] = (u[rows, cols] * (mix + gb_ref[p])).astype(og_ref.dtype)


def _inproj(x, w, *, sample, tm):
    rows = x.shape[0]
    grid = (rows // tm,)
    row = lambda n: pl.BlockSpec((tm, n), lambda i: (i, 0))
    if sample:
        gw_spec, gb_spec = _full((1, GMLP_WIDTH)), _full((1, GMLP_WIDTH))
        gw, gb = w["g_diag_w"], w["g_diag_b"]
    else:
        gw_spec, gb_spec = _full((GMLP_GROUPS, CHUNK, CHUNK)), _full((GMLP_WIDTH // LANES, CHUNK, LANES))
        gw, gb = w["g_tril"], w["g_bias"]
    out_shape = (
        jax.ShapeDtypeStruct((rows, NSA_WIDTH), BF16),
        jax.ShapeDtypeStruct((3 * KV_COLS, rows), F32),
        jax.ShapeDtypeStruct((2 * KV_COLS, rows), BF16),
        jax.ShapeDtypeStruct((rows, KV_COLS), BF16),
        jax.ShapeDtypeStruct((rows, LANES), F32),
        jax.ShapeDtypeStruct((rows, GMLP_WIDTH), BF16),
        jax.ShapeDtypeStruct((rows, GMLP_WIDTH), F32),
    )
    return pl.pallas_call(
        functools.partial(_inproj_kernel, sample=sample),
        grid=grid,
        in_specs=[row(D_MODEL), _full((D_MODEL, NSA_WIDTH)), _full((3 * KV_COLS, D_MODEL)),
                  _full((D_MODEL, KV_COLS)), _full((D_MODEL, LANES)), _full((D_MODEL, GMLP_WIDTH)),
                  _full((D_MODEL, GMLP_WIDTH)), _full((1, GMLP_WIDTH)), _full((1, GMLP_WIDTH)),
                  gw_spec, gb_spec],
        out_specs=(row(NSA_WIDTH), pl.BlockSpec((3 * KV_COLS, tm), lambda i: (0, i)),
                   pl.BlockSpec((2 * KV_COLS, tm), lambda i: (0, i)), row(KV_COLS),
                   row(LANES), row(GMLP_WIDTH), row(GMLP_WIDTH)),
        out_shape=out_shape,
        compiler_params=_cparams(("parallel",)),
        name="inproj_sample" if sample else "inproj_prompt",
    )(x, w["wq"], w["wkvt"], w["wkvc"], w["wg"], w["wu"], w["wv"], w["lng"], w["lnb"], gw, gb)


def _outproj_kernel(oa_ref, og_ref, x_ref, wa_ref, wg_ref, g_ref, b_ref, o_ref, *, alpha):
    mixed = _dot(oa_ref[...], wa_ref[...]) + _dot(og_ref[...], wg_ref[...])
    o_ref[...] = _ln(alpha * x_ref[...] + mixed, g_ref[...], b_ref[...])


def _outproj(oa, og, x, w, *, alpha, tm, name):
    rows = x.shape[0]
    row = lambda n: pl.BlockSpec((tm, n), lambda i: (i, 0))
    return pl.pallas_call(
        functools.partial(_outproj_kernel, alpha=alpha),
        grid=(rows // tm,),
        in_specs=[row(NSA_WIDTH), row(GMLP_WIDTH), row(D_MODEL), _full((NSA_WIDTH, D_MODEL)),
                  _full((GMLP_WIDTH, D_MODEL)), _full((1, D_MODEL)), _full((1, D_MODEL))],
        out_specs=row(D_MODEL),
        out_shape=jax.ShapeDtypeStruct((rows, D_MODEL), F32),
        compiler_params=_cparams(("parallel",)),
        name=name,
    )(oa, og, x, w["wo_a"], w["wo_g"], w["ln1_g"], w["ln1_b"])


def _ffn_kernel(x_ref, w1_ref, w2_ref, g_ref, b_ref, o_ref, *, alpha):
    x = x_ref[...]
    h = _dot(x.astype(BF16), w1_ref[...])
    act = (jax.nn.silu(h[:, :D_FF]) * h[:, D_FF:]).astype(BF16)
    y = _dot(act, w2_ref[...])
    o_ref[...] = _ln(alpha * x + y, g_ref[...], b_ref[...])


def _ffn_dense(x, w1, w2, g, b, *, alpha, tm, name):
    rows = x.shape[0]
    row = lambda n: pl.BlockSpec((tm, n), lambda i: (i, 0))
    return pl.pallas_call(
        functools.partial(_ffn_kernel, alpha=alpha),
        grid=(rows // tm,),
        in_specs=[row(D_MODEL), _full((D_MODEL, 2 * D_FF)), _full((D_FF, D_MODEL)),
                  _full((1, D_MODEL)), _full((1, D_MODEL))],
        out_specs=row(D_MODEL),
        out_shape=jax.ShapeDtypeStruct((rows, D_MODEL), F32),
        compiler_params=_cparams(("parallel",)),
        name=name,
    )(x, w1, w2, g, b)


def _finish_compress(ac, w2):
    n = ac.shape[0]
    a, c = ac[:, :KV_COLS], ac[:, KV_COLS:]
    row = lax.broadcasted_iota(jnp.int32, (n, KV_COLS), 0)
    c_next = jnp.where(row == n - 1, 0.0, pltpu.roll(c, n - 1, 0))
    h = jax.nn.gelu(a + c_next)
    return _dot(h.astype(BF16), w2)


def _compress_prompt_kernel(xs_ref, w1_ref, w2_ref, o_ref):
    o_ref[...] = _finish_compress(_dot(xs_ref[...], w1_ref[...]), w2_ref[...])


def _compress_prompt(kvc_rows, w):
    t = kvc_rows.shape[0]
    n_sub = t // CMP_STRIDE
    xs = kvc_rows.reshape(n_sub, CMP_STRIDE * KV_COLS)
    return pl.pallas_call(
        _compress_prompt_kernel,
        grid=(1,),
        in_specs=[_full(xs.shape), _full(w["phi1_rows"].shape), _full(w["phi2_big"].shape)],
        out_specs=_full((n_sub, KV_COLS)),
        out_shape=jax.ShapeDtypeStruct((n_sub, KV_COLS), F32),
        compiler_params=_cparams(("arbitrary",)),
        name="compress_prompt",
    )(xs, w["phi1_rows"], w["phi2_big"])


NEG_INF = float("-inf")


def _slope(head):
    return 2.0 ** -(head + 1)


def _cmp_probs(s, dist, valid, slope):
    s = jnp.where(valid, s - slope * dist, NEG_INF)
    m = jnp.max(s, axis=-1, keepdims=True)
    m = jnp.where(jnp.isfinite(m), m, 0.0)
    e = jnp.exp(s - m)
    return e / jnp.maximum(jnp.sum(e, axis=-1, keepdims=True), 1e-30)


def _select_blocks(pooled, cur, n_sel):
    blk = lax.broadcasted_iota(jnp.int32, pooled.shape, 1)
    blkf = blk.astype(F32)
    forced = (blk == 0) | (blk == cur) | (blk == cur - 1)
    work = jnp.where(blk <= cur, jnp.where(forced, jnp.inf, pooled), NEG_INF)
    sel = jnp.zeros(pooled.shape, F32)
    picks = []
    for _ in range(min(N_SEL_BLOCKS, n_sel)):
        m = jnp.max(work, axis=-1, keepdims=True)
        idx = jnp.min(jnp.where(work == m, blkf, 1e9), axis=-1, keepdims=True)
        pick = blkf == idx
        sel = jnp.maximum(sel, jnp.where(pick, jnp.where(m > NEG_INF, 1.0, 0.0), 0.0))
        work = jnp.where(pick, NEG_INF, work)
        picks.append(idx)
    return sel, picks


def _select_mask(pooled, cur, tri):
    n_top = min(N_SEL_BLOCKS, pooled.shape[1])
    blk = lax.broadcasted_iota(jnp.int32, pooled.shape, 1)
    forced = (blk == 0) | (blk == cur) | (blk == cur - 1)
    work = jnp.where(blk <= cur, jnp.where(forced, jnp.inf, pooled), NEG_INF)
    never = float(n_top)
    rnd = jnp.full(pooled.shape, never + 1.0, F32)
    counts = []
    for r in range(n_top):
        m = jnp.max(work, axis=-1, keepdims=True)
        pick = work == m
        live = m > NEG_INF
        rnd = jnp.where(pick, jnp.where(live, float(r), rnd), rnd)
        counts.append(jnp.sum(jnp.where(pick, jnp.where(live, 1.0, 0.0), 0.0), axis=-1, keepdims=True))
        work = jnp.where(pick, NEG_INF, work)
    taken = jnp.zeros((pooled.shape[0], 1), F32)
    last = jnp.full((pooled.shape[0], 1), never, F32)
    taken_before_last = jnp.zeros((pooled.shape[0], 1), F32)
    for r in range(n_top):
        crosses = (taken < n_top) & (taken + counts[r] >= n_top)
        last = jnp.where(crosses, float(r), last)
        taken_before_last = jnp.where(crosses, taken, taken_before_last)
        taken = taken + counts[r]
    in_last = jnp.where(rnd == last, 1.0, 0.0)
    lower_ties = _dot(in_last.astype(BF16), tri)
    return jnp.where(rnd < last, 1.0, in_last * jnp.where(taken_before_last + lower_ties < n_top, 1.0, 0.0))


def _cmp_select_kernel(q_ref, kbd_ref, vl_ref, vr_ref, cpos_ref, tri_ref, oc_ref, unsel_ref, flag_ref, *,
                       n_sel, tq):
    n_cmp = 4 * n_sel
    q0 = pl.program_id(0) * tq
    qpos = q0 + lax.broadcasted_iota(jnp.int32, (tq, n_cmp), 0)
    dist = qpos.astype(F32) - cpos_ref[...]
    valid = dist >= 0
    cur = lax.shift_right_logical(q0 + lax.broadcasted_iota(jnp.int32, (tq, n_sel), 0), 6)
    for kvh in range(N_KV_HEADS):
        imp = jnp.zeros((tq, n_cmp), F32)
        for p in range(GROUP // 2):
            cols = slice((kvh * 2 + p) * LANES, (kvh * 2 + p + 1) * LANES)
            s2 = _dot(q_ref[:, cols], kbd_ref[kvh])
            o_pair = jnp.zeros((tq, LANES), F32)
            for j, v_ref in enumerate((vl_ref, vr_ref)):
                head = kvh * GROUP + 2 * p + j
                pr = _cmp_probs(s2[:, j * n_cmp:(j + 1) * n_cmp], dist, valid, _slope(head))
                imp = imp + pr
                o_pair = o_pair + _dot(pr.astype(BF16), v_ref[kvh])
            oc_ref[:, cols] = o_pair
        pooled = imp[:, :n_sel]
        for r in range(1, 4):
            pooled = pooled + imp[:, r * n_sel:(r + 1) * n_sel]
        sel = _select_mask(pooled, cur, tri_ref[...])
        unsel_ref[kvh] = (1.0 - sel).T
        flag_ref[0, kvh:kvh + 1, :] = jnp.max(sel, axis=0, keepdims=True).astype(jnp.int32)


def _cmp_select(q, cmp_kv, *, tq):
    t = q.shape[0]
    n_cmp = cmp_kv.shape[0]
    n_sel = n_cmp // 4
    perm = cmp_kv.reshape(n_sel, 4, N_KV_HEADS, 2, HEAD_DIM).transpose(2, 3, 1, 0, 4)
    perm = perm.reshape(N_KV_HEADS, 2, n_cmp, HEAD_DIM)
    kc, vc = perm[:, 0], perm[:, 1]
    kct = jnp.swapaxes(kc, 1, 2).astype(BF16)
    z = jnp.zeros_like(kct)
    kbd = jnp.concatenate([jnp.concatenate([kct, z], axis=2), jnp.concatenate([z, kct], axis=2)], axis=1)
    zv = jnp.zeros_like(vc)
    vl = jnp.concatenate([vc, zv], axis=2).astype(BF16)
    vr = jnp.concatenate([zv, vc], axis=2).astype(BF16)
    col = jnp.arange(n_cmp)
    cpos = ((4 * (col % n_sel) + col // n_sel) * CMP_STRIDE + (CMP_LEN - 1)).astype(F32)[None, :]
    tri = jnp.asarray(np.triu(np.ones((n_sel, n_sel), np.float32), 1), BF16)
    row = lambda n: pl.BlockSpec((tq, n), lambda i: (i, 0))
    return pl.pallas_call(
        functools.partial(_cmp_select_kernel, n_sel=n_sel, tq=tq),
        grid=(t // tq,),
        in_specs=[row(NSA_WIDTH), _full(kbd.shape), _full(vl.shape), _full(vr.shape), _full(cpos.shape),
                  _full(tri.shape)],
        out_specs=(row(NSA_WIDTH), pl.BlockSpec((N_KV_HEADS, n_sel, tq), lambda i: (0, 0, i)),
                   pl.BlockSpec((1, N_KV_HEADS, n_sel), lambda i: (i, 0, 0))),
        out_shape=(jax.ShapeDtypeStruct((t, NSA_WIDTH), F32),
                   jax.ShapeDtypeStruct((N_KV_HEADS, n_sel, t), F32),
                   jax.ShapeDtypeStruct((t // tq, N_KV_HEADS, n_sel), jnp.int32)),
        compiler_params=_cparams(("parallel",)),
        name="cmp_select_prompt",
    )(q, kbd, vl, vr, cpos, tri)


MASKED = -1e30


MASK_BIG = 2.0 ** 100
M_FLOOR = -(2.0 ** 99)
KIND_PLAIN, KIND_CAUSAL, KIND_EDGE, KIND_DUMMY = 0, 1, 2, 3
N_FEAT = 16
SLOTS = 3


def _unit_code(kind, chunk):
    return kind * 256 + chunk


def _bias_rows():
    lane = np.arange(2 * LANES)
    key = lane % LANES
    out = np.zeros((N_KV_HEADS, 4, GROUP // 2, N_FEAT, 2 * LANES), np.float32)
    for kvh in range(N_KV_HEADS):
        for pair in range(GROUP // 2):
            slope = 2.0 ** -(kvh * GROUP + 2 * pair + lane // LANES + 1.0)
            for cm in range(4):
                for f in range(8):
                    out[kvh, cm, pair, f] = np.where(f == 2 * cm + key // SLC_BLOCK, -MASK_BIG, 0.0)
                out[kvh, cm, pair, 8] = -slope
                out[kvh, cm, pair, 9] = slope * key
                out[kvh, cm, pair, 10] = slope * LANES
    return jnp.asarray(out, BF16)


def _gate_expansion():
    out = np.zeros((LANES, 3 * NSA_WIDTH), np.float32)
    col = np.arange(3 * NSA_WIDTH)
    out[col // HEAD_DIM, col] = 1.0
    return jnp.asarray(out, BF16)


def _unit_scores(code, i, kvh, branch, q_ref, kv_ref, unselt_ref, rhsx_ref, mb_ref, const8, rp_ref, vw_ref,
                 xt_ref):
    kind = lax.shift_right_logical(code, 8)
    c = code & 255
    lanes = pl.ds(pl.multiple_of(c * LANES, LANES), LANES)
    row0 = branch * KV_COLS + kvh * 2 * HEAD_DIM
    kt = kv_ref[row0:row0 + HEAD_DIM, lanes]
    vt = kv_ref[row0 + HEAD_DIM:row0 + 2 * HEAD_DIM, lanes]
    vw_ref[:HEAD_DIM, :LANES] = vt
    vw_ref[LANES + HEAD_DIM:, LANES:] = vt
    dummy = jnp.where(kind == KIND_DUMMY, 1.0, 0.0)
    if branch == 0:
        blocks8 = pl.ds(pl.multiple_of(lax.shift_right_logical(c, 2) * 8, 8), 8)
        xt_ref[0:8, :] = unselt_ref[kvh, blocks8, :] * (1.0 - dummy) + dummy
    else:
        xt_ref[0:8, :] = jnp.zeros((8, xt_ref.shape[1]), F32) + dummy
    row8 = lax.broadcasted_iota(jnp.int32, const8.shape, 0)
    xt_ref[8:16, :] = const8 + jnp.where(row8 == 2, (c - i).astype(F32), 0.0)
    x = xt_ref[...].T.astype(BF16)
    scores = []
    for pair in range(GROUP // 2):
        rp_ref[pair, :HEAD_DIM, :LANES] = kt
        rp_ref[pair, HEAD_DIM:LANES, LANES:] = kt
        rp_ref[pair, LANES:LANES + N_FEAT, :] = rhsx_ref[kvh, c & 3, pair]
        cols = slice((kvh * 2 + pair) * LANES, (kvh * 2 + pair + 1) * LANES)
        scores.append(_dot(jnp.concatenate([q_ref[:, cols], x], axis=1), rp_ref[pair]))
    return scores, mb_ref[jnp.where(kind == KIND_DUMMY, 0, kind)]


def _unit_probs(scores, mb, m_ref):
    out = []
    for pair, s2 in enumerate(scores):
        ps, alphas = [], []
        for j in range(2):
            h = 2 * pair + j
            s = s2[:, j * LANES:(j + 1) * LANES] + mb
            m_old = m_ref[h]
            m_new = jnp.maximum(m_old, jnp.max(s, axis=-1, keepdims=True))
            ps.append(jnp.exp(s - m_new).astype(BF16))
            alphas.append(jnp.exp(m_old - m_new))
            m_ref[h] = m_new
        out.append((jnp.concatenate(ps, axis=1), jnp.concatenate(alphas, axis=1)))
    return out


def _unit_accumulate(probs, vw_ref, acc_ref):
    for pair, (p2, alpha2) in enumerate(probs):
        acc_ref[pair] = alpha2 * acc_ref[pair] + _dot_nt(p2, vw_ref[...])


def _build_unit_lists(kvh, i, flag_ref, list_ref):
    a, b, w = kvh * SLOTS, kvh * SLOTS + 1, kvh * SLOTS + 2
    n_win = WINDOW // LANES
    list_ref[a, 0] = _unit_code(KIND_CAUSAL, i)
    list_ref[w, 0] = _unit_code(KIND_CAUSAL, i)
    n_w = jnp.int32(1)
    for d in range(1, n_win + 1):
        list_ref[w, n_w] = _unit_code(KIND_EDGE if d == n_win else KIND_PLAIN, jnp.maximum(i - d, 0))
        n_w = n_w + (i >= d).astype(jnp.int32)

    def scan(c, counts):
        n_a, n_b = counts
        code = _unit_code(KIND_PLAIN, c)
        list_ref[a, n_a] = code
        list_ref[b, n_b] = code
        picked = ((flag_ref[0, kvh, 2 * c] + flag_ref[0, kvh, 2 * c + 1]) > 0).astype(jnp.int32)
        to_a = (n_a <= n_b).astype(jnp.int32)
        return n_a + picked * to_a, n_b + picked * (1 - to_a)

    n_a, n_b = lax.fori_loop(0, i, scan, (jnp.int32(1), jnp.int32(0)))
    return n_a, n_b, n_w


def _attend_result(acc, lane):
    l_even = jnp.maximum(acc[:, HEAD_DIM:HEAD_DIM + 1], 1e-30)
    l_odd = jnp.maximum(acc[:, LANES:LANES + 1], 1e-30)
    return jnp.where(lane < HEAD_DIM, acc[:, :LANES] / l_even, acc[:, LANES:] / l_odd)


def _slc_win_kernel(flag_ref, q_ref, kv_ref, unselt_ref, rhsx_ref, egate_ref, gates_ref, oc_ref, o_ref,
                    list_ref, rp_ref, vw_ref, xt_ref, mb_ref, m_ref, acc_ref, *, tq):
    i = pl.program_id(0)
    n_slots = N_KV_HEADS * SLOTS
    row = lax.broadcasted_iota(jnp.int32, (tq, LANES), 0)
    lane = lax.broadcasted_iota(jnp.int32, (tq, LANES), 1)

    @pl.when(i == 0)
    def _():
        mb_ref[KIND_PLAIN] = jnp.zeros((tq, LANES), F32)
        mb_ref[KIND_CAUSAL] = jnp.where(lane <= row, 0.0, -MASK_BIG)
        mb_ref[KIND_EDGE] = jnp.where(lane >= row, 0.0, -MASK_BIG)
        rp_ref[...] = jnp.zeros(rp_ref.shape, rp_ref.dtype)
        vrow = lax.broadcasted_iota(jnp.int32, (2 * LANES, 2 * LANES), 0)
        vcol = lax.broadcasted_iota(jnp.int32, (2 * LANES, 2 * LANES), 1)
        ones_rows = ((vrow >= HEAD_DIM) & (vrow < LANES + HEAD_DIM)) & ((vrow < LANES) == (vcol < LANES))
        for s in range(n_slots):
            vw_ref[s] = jnp.where(ones_rows, 1.0, 0.0).astype(vw_ref.dtype)
        xt_ref[...] = jnp.zeros(xt_ref.shape, F32)

    m_ref[...] = jnp.full(m_ref.shape, M_FLOOR, F32)
    acc_ref[...] = jnp.zeros(acc_ref.shape, F32)
    row8 = lax.broadcasted_iota(jnp.int32, (8, tq), 0)
    q_index = lax.broadcasted_iota(jnp.int32, (8, tq), 1).astype(F32)
    const8 = jnp.where(row8 == 0, q_index, jnp.where(row8 == 1, 1.0, 0.0))

    counts = []
    for kvh in range(N_KV_HEADS):
        counts.extend(_build_unit_lists(kvh, i, flag_ref, list_ref))
    n_iter = functools.reduce(jnp.maximum, counts)
    for s in range(n_slots):
        def pad(t, carry, s=s):
            list_ref[s, t] = _unit_code(KIND_DUMMY, 0)
            return carry
        lax.fori_loop(counts[s], n_iter, pad, 0)

    def body(t, carry):
        for kvh in range(N_KV_HEADS):
            slots = range(kvh * SLOTS, (kvh + 1) * SLOTS)
            scored = [_unit_scores(list_ref[s, t], i, kvh, int(s % SLOTS == SLOTS - 1), q_ref, kv_ref, unselt_ref,
                                   rhsx_ref, mb_ref, const8, rp_ref.at[s], vw_ref.at[s], xt_ref.at[s])
                      for s in slots]
            probs = [_unit_probs(scores, mb, m_ref.at[s]) for s, (scores, mb) in zip(slots, scored)]
            for s, pr in zip(slots, probs):
                _unit_accumulate(pr, vw_ref.at[s], acc_ref.at[s])
        return carry

    lax.fori_loop(0, n_iter, body, 0)

    gates = gates_ref[...]
    g_hi = gates.astype(BF16)
    g_lo = (gates - g_hi.astype(F32)).astype(BF16)
    g = _dot(g_hi, egate_ref[...]) + _dot(g_lo, egate_ref[...])
    for kvh in range(N_KV_HEADS):
        a, b, w = kvh * SLOTS, kvh * SLOTS + 1, kvh * SLOTS + 2
        for pair in range(GROUP // 2):
            m_a = jnp.concatenate([m_ref[a, 2 * pair], m_ref[a, 2 * pair + 1]], axis=1)
            m_b = jnp.concatenate([m_ref[b, 2 * pair], m_ref[b, 2 * pair + 1]], axis=1)
            m = jnp.maximum(m_a, m_b)
            slc = jnp.exp(m_a - m) * acc_ref[a, pair] + jnp.exp(m_b - m) * acc_ref[b, pair]
            c0 = (kvh * 2 + pair) * LANES
            o_ref[:, c0:c0 + LANES] = (
                g[:, c0:c0 + LANES] * oc_ref[:, c0:c0 + LANES]
                + g[:, NSA_WIDTH + c0:NSA_WIDTH + c0 + LANES] * _attend_result(slc, lane)
                + g[:, 2 * NSA_WIDTH + c0:2 * NSA_WIDTH + c0 + LANES] * _attend_result(acc_ref[w, pair], lane)
            ).astype(o_ref.dtype)


def _slc_win(q, kvtb, unselt, flags, gates, o_cmp, *, tq):
    t = q.shape[0]
    n_sel = unselt.shape[1]
    rhsx, egate = _bias_rows(), _gate_expansion()
    n_slots = N_KV_HEADS * SLOTS
    max_units = 8 + t // LANES
    row = lambda n: pl.BlockSpec((tq, n), lambda i: (i, 0))
    return pl.pallas_call(
        functools.partial(_slc_win_kernel, tq=tq),
        grid=(t // tq,),
        in_specs=[pl.BlockSpec((1, N_KV_HEADS, n_sel), lambda i: (i, 0, 0), memory_space=pltpu.SMEM),
                  row(NSA_WIDTH), _full(kvtb.shape),
                  pl.BlockSpec((N_KV_HEADS, n_sel, tq), lambda i: (0, 0, i)), _full(rhsx.shape),
                  _full(egate.shape), row(LANES), row(NSA_WIDTH)],
        out_specs=row(NSA_WIDTH),
        out_shape=jax.ShapeDtypeStruct((t, NSA_WIDTH), BF16),
        scratch_shapes=[pltpu.SMEM((n_slots, max_units), jnp.int32),
                        pltpu.VMEM((n_slots, GROUP // 2, 2 * LANES, 2 * LANES), BF16),
                        pltpu.VMEM((n_slots, 2 * LANES, 2 * LANES), BF16),
                        pltpu.VMEM((n_slots, LANES, tq), F32),
                        pltpu.VMEM((3, tq, LANES), F32),
                        pltpu.VMEM((n_slots, GROUP, tq, LANES), F32),
                        pltpu.VMEM((n_slots, GROUP // 2, tq, 2 * LANES), F32)],
        compiler_params=_cparams(("arbitrary",)),
        name="slc_win_prompt",
    )(flags, q, kvtb, unselt, rhsx, egate, gates, o_cmp)


PAGES_PER_STEP = 32


def _compress_sample_kernel(pt_ref, *refs, n_pages):
    page_refs, (perm_ref, w1_ref, ac_ref, xs_ref) = refs[:n_pages], refs[n_pages:]
    per_page = PAGE_SIZE // CMP_STRIDE
    for p in range(n_pages):
        page_t = page_refs[p][0, 0].reshape(KV_COLS, PAGE_SIZE).astype(BF16)
        xs_ref[p] = _dot_nt(perm_ref[...], page_t)
    lhs = jnp.concatenate(
        [xs_ref[:, l * per_page:(l + 1) * per_page, :].reshape(n_pages * per_page, KV_COLS).astype(BF16)
         for l in range(CMP_STRIDE)], axis=1)
    ac_ref[0] = _dot(lhs, w1_ref[...])


def _compress_sample(cache_t, page_table, w, layer):
    b, n_pg = page_table.shape
    p = min(PAGES_PER_STEP, n_pg)
    n = p * PAGE_SIZE // CMP_STRIDE
    w1 = w["phi1_rows"]
    rows = np.arange(PAGE_SIZE)
    perm = np.zeros((PAGE_SIZE, PAGE_SIZE), np.float32)
    perm[rows, (rows % (PAGE_SIZE // CMP_STRIDE)) * CMP_STRIDE + rows // (PAGE_SIZE // CMP_STRIDE)] = 1.0
    perm = jnp.asarray(perm, BF16)

    def page_spec(j):
        return pl.BlockSpec((1, 1, N_KV_HEADS, 2, HEAD_DIM, PAGE_SIZE),
                            lambda bi, c, pt: (layer, pt[bi, c * p + j], 0, 0, 0, 0))

    return pl.pallas_call(
        functools.partial(_compress_sample_kernel, n_pages=p),
        grid_spec=pltpu.PrefetchScalarGridSpec(
            num_scalar_prefetch=1,
            grid=(b, n_pg // p),
            in_specs=[page_spec(j) for j in range(p)]
            + [pl.BlockSpec(perm.shape, lambda bi, c, pt: (0, 0)),
               pl.BlockSpec(w1.shape, lambda bi, c, pt: (0, 0))],
            out_specs=pl.BlockSpec((1, n, 2 * KV_COLS), lambda bi, c, pt: (bi, c, 0)),
            scratch_shapes=[pltpu.VMEM((p, PAGE_SIZE, KV_COLS), F32)],
        ),
        out_shape=jax.ShapeDtypeStruct((b, n_pg * PAGE_SIZE // CMP_STRIDE, 2 * KV_COLS), F32),
        compiler_params=_cparams(("parallel", "parallel")),
        name="compress_sample",
    )(page_table, *([cache_t] * p), perm, w1)


def _finish_sample_kernel(ac_ref, w2_ref, kvt_ref, res_ref):
    res = _finish_compress(ac_ref[0], w2_ref[...])
    n_sel = res.shape[0] // 4
    for kvh in range(N_KV_HEADS):
        res_ref[kvh] = res[:, kvh * LANES:(kvh + 1) * LANES]
        for r in range(4):
            kv = res_ref[kvh, pl.ds(r, n_sel, stride=4), :]
            kvt_ref[0, kvh, :, r * n_sel:(r + 1) * n_sel] = kv.T.astype(kvt_ref.dtype)


def _finish_sample(ac, w):
    b, n_cmp, _ = ac.shape
    return pl.pallas_call(
        _finish_sample_kernel,
        grid=(b,),
        in_specs=[pl.BlockSpec((1, n_cmp, 2 * KV_COLS), lambda i: (i, 0, 0)), _full((KV_COLS, KV_COLS))],
        out_specs=pl.BlockSpec((1, N_KV_HEADS, LANES, n_cmp), lambda i: (i, 0, 0, 0)),
        out_shape=jax.ShapeDtypeStruct((b, N_KV_HEADS, LANES, n_cmp), BF16),
        scratch_shapes=[pltpu.VMEM((N_KV_HEADS, n_cmp, LANES), F32)],
        compiler_params=_cparams(("parallel",)),
        name="finish_compress_sample",
    )(ac, w["phi2_big"])


def _cmp_select_sample_kernel(q_ref, kvt_ref, cpos_ref, slope_ref, oc_ref, idx_ref, *, n_sel, t_past):
    n_cmp = 4 * n_sel
    q = q_ref[0].astype(BF16)
    group0 = lax.broadcasted_iota(jnp.int32, (N_HEADS, n_cmp), 0) < GROUP
    s = jnp.where(group0, _dot(q, kvt_ref[0, 0, :HEAD_DIM, :]), _dot(q, kvt_ref[0, 1, :HEAD_DIM, :]))
    dist = jnp.broadcast_to(float(t_past) - cpos_ref[...], (N_HEADS, n_cmp))
    pr = _cmp_probs(s, dist, dist >= 0, slope_ref[:, 0:1])
    prb = pr.astype(BF16)
    oc_ref[0] = jnp.where(group0[:, :HEAD_DIM], _dot_nt(prb, kvt_ref[0, 0, HEAD_DIM:, :]),
                          _dot_nt(prb, kvt_ref[0, 1, HEAD_DIM:, :]))
    imp = jnp.where(group0, jnp.sum(pr[:GROUP], axis=0, keepdims=True),
                    jnp.sum(pr[GROUP:], axis=0, keepdims=True))
    pooled = imp[:, :n_sel]
    for r in range(1, 4):
        pooled = pooled + imp[:, r * n_sel:(r + 1) * n_sel]
    pooled = jnp.concatenate([pooled, jnp.zeros((N_HEADS, LANES), F32)], axis=1)
    cur = jnp.full(pooled.shape, t_past // SLC_BLOCK, jnp.int32)
    _, picks = _select_blocks(pooled, cur, pooled.shape[1])
    lane = lax.broadcasted_iota(jnp.int32, (N_HEADS, LANES), 1)
    idx = jnp.zeros((N_HEADS, LANES), F32)
    for it, pick in enumerate(picks):
        idx = jnp.where(lane == it, pick, idx)
    idx_ref[0] = idx.astype(jnp.int32)


def _cmp_select_sample(q8, kvt_cmp, slopes, *, t_past):
    b = q8.shape[0]
    n_cmp = kvt_cmp.shape[-1]
    n_sel = n_cmp // 4
    col = jnp.arange(n_cmp)
    cpos = ((4 * (col % n_sel) + col // n_sel) * CMP_STRIDE + (CMP_LEN - 1)).astype(F32)[None, :]
    return pl.pallas_call(
        functools.partial(_cmp_select_sample_kernel, n_sel=n_sel, t_past=t_past),
        grid=(b,),
        in_specs=[pl.BlockSpec((1, N_HEADS, HEAD_DIM), lambda i: (i, 0, 0)),
                  pl.BlockSpec((1, N_KV_HEADS, LANES, n_cmp), lambda i: (i, 0, 0, 0)),
                  _full(cpos.shape), _full(slopes.shape)],
        out_specs=(pl.BlockSpec((1, N_HEADS, HEAD_DIM), lambda i: (i, 0, 0)),
                   pl.BlockSpec((1, N_HEADS, LANES), lambda i: (i, 0, 0))),
        out_shape=(jax.ShapeDtypeStruct((b, N_HEADS, HEAD_DIM), F32),
                   jax.ShapeDtypeStruct((b, N_HEADS, LANES), jnp.int32)),
        compiler_params=_cparams(("parallel",)),
        name="cmp_select_sample",
    )(q8, kvt_cmp, cpos, slopes)


def _softmax_step(state, s, valid, vt):
    m, l, acc = state
    s = jnp.where(valid, s, MASKED)
    m_new = jnp.maximum(m, jnp.max(s, axis=-1, keepdims=True))
    p = jnp.where(valid, jnp.exp(s - m_new), 0.0)
    a = jnp.exp(m - m_new)
    return m_new, a * l + jnp.sum(p, axis=-1, keepdims=True), a * acc + _dot_nt(p.astype(BF16), vt)


def _softmax_new_row(state, q, k_rows, v_rows):
    m, l, acc = state
    s = jnp.sum(q * k_rows, axis=-1, keepdims=True)
    m_new = jnp.maximum(m, s)
    p = jnp.exp(s - m_new)
    a = jnp.exp(m - m_new)
    l, acc = a * l + p, a * acc + p * v_rows
    return acc / jnp.maximum(l, 1e-30)


def _slc_win_sample_kernel(sel_ref, pt_ref, q_ref, *refs, n_top, t_past):
    n_pages = N_KV_HEADS * n_top
    page_refs = refs[:n_pages]
    win_ref, snew_ref, wnew_ref, gates_ref, oc_ref, slope_ref, o_ref = refs[n_pages:]
    b = pl.program_id(0)
    q = q_ref[0]
    qb = q.astype(BF16)
    slope = slope_ref[:, 0:1]
    row_kvh = lax.shift_right_logical(lax.broadcasted_iota(jnp.int32, (N_HEADS, LANES), 0), 2)
    lane = lax.broadcasted_iota(jnp.int32, (N_HEADS, LANES), 1)
    lane_half = lax.shift_right_logical(lane, 6)
    init = (jnp.full((N_HEADS, 1), MASKED, F32), jnp.zeros((N_HEADS, 1), F32),
            jnp.zeros((N_HEADS, HEAD_DIM), F32))

    state = init
    for kvh in range(N_KV_HEADS):
        for j in range(n_top):
            blk = sel_ref[b, kvh * n_top + j]
            page = page_refs[kvh * n_top + j]
            kpos = lax.shift_right_logical(blk, 1) * PAGE_SIZE + lane
            dist = (t_past - kpos).astype(F32)
            valid = (lane_half == (blk & 1)) & (row_kvh == kvh) & (blk < t_past // SLC_BLOCK)
            s = _dot(qb, page[0, 0, 0, 0].astype(BF16)) - slope * dist
            state = _softmax_step(state, s, valid, page[0, 0, 0, 1].astype(BF16))
    o_slc = _softmax_new_row(state, q, snew_ref[0, 0], snew_ref[0, 1])

    state = init
    w = win_ref.shape[-1]
    wlane = lax.broadcasted_iota(jnp.int32, (N_HEADS, w), 1)
    wdist = (w - wlane).astype(F32)
    wrow_kvh = lax.shift_right_logical(lax.broadcasted_iota(jnp.int32, (N_HEADS, w), 0), 2)
    for kvh in range(N_KV_HEADS):
        s = _dot(qb, win_ref[0, 0, kvh, 0].astype(BF16)) - slope * wdist
        valid = (wrow_kvh == kvh) & (wdist <= WINDOW)
        state = _softmax_step(state, s, valid, win_ref[0, 0, kvh, 1].astype(BF16))
    o_win = _softmax_new_row(state, q, wnew_ref[0, 0], wnew_ref[0, 1])

    o_ref[0] = gates_ref[0, 0] * oc_ref[0] + gates_ref[0, 1] * o_slc + gates_ref[0, 2] * o_win


def _slc_win_sample(sel, page_table, q8, slc_t, win_t, slc_new, win_new, gates, o_cmp, slopes, layer, *,
                    t_past):
    b, n_pg = page_table.shape
    n_top = sel.shape[1] // N_KV_HEADS
    w = win_t.shape[-1]

    def page_spec(kvh, j):
        def index(bi, sel_ref, pt_ref):
            pg = jnp.minimum(lax.shift_right_logical(sel_ref[bi, kvh * n_top + j], 1), n_pg - 1)
            return (layer, pt_ref[bi, pg], kvh, 0, 0, 0)
        return pl.BlockSpec((1, 1, 1, 2, HEAD_DIM, PAGE_SIZE), index)

    per_seq = lambda *tail: pl.BlockSpec((1,) + tail, lambda bi, s, p: (bi,) + (0,) * len(tail))
    return pl.pallas_call(
        functools.partial(_slc_win_sample_kernel, n_top=n_top, t_past=t_past),
        grid_spec=pltpu.PrefetchScalarGridSpec(
            num_scalar_prefetch=2,
            grid=(b,),
            in_specs=[per_seq(N_HEADS, HEAD_DIM)]
            + [page_spec(kvh, j) for kvh in range(N_KV_HEADS) for j in range(n_top)]
            + [pl.BlockSpec((1, 1, N_KV_HEADS, 2, HEAD_DIM, w), lambda bi, s, p: (layer, bi, 0, 0, 0, 0)),
               per_seq(2, N_HEADS, HEAD_DIM), per_seq(2, N_HEADS, HEAD_DIM), per_seq(3, N_HEADS, 1),
               per_seq(N_HEADS, HEAD_DIM), pl.BlockSpec(slopes.shape, lambda bi, s, p: (0, 0))],
            out_specs=per_seq(N_HEADS, HEAD_DIM),
        ),
        out_shape=jax.ShapeDtypeStruct((b, N_HEADS, HEAD_DIM), F32),
        compiler_params=_cparams(("parallel",)),
        name="slc_win_sample",
    )(sel, page_table, q8, *([slc_t] * (N_KV_HEADS * n_top)), win_t, slc_new, win_new, gates, o_cmp, slopes)


MOE_TOKENS = 256
MOE_SLOTS = 256


def _n_slot_blocks(n_tokens):
    return -(-(2 * n_tokens + N_EXPERTS * (MOE_SLOTS - 1)) // MOE_SLOTS)


def _route_kernel(x_ref, wr_ref, dest_ref, wts_ref, blk_ref, count_ref, base_ref, *, n_blocks):
    phase, i = pl.program_id(0), pl.program_id(1)
    tt = x_ref.shape[0]
    logits = lax.dot_general(wr_ref[...], x_ref[...], (((1,), (1,)), ((), ())),
                             precision=lax.Precision.HIGHEST, preferred_element_type=F32)
    row = lax.broadcasted_iota(jnp.int32, (N_EXPERTS, tt), 0)
    rowf = row.astype(F32)
    m1 = jnp.max(logits, axis=0, keepdims=True)
    e1 = jnp.min(jnp.where(logits == m1, rowf, float(N_EXPERTS)), axis=0, keepdims=True)
    rest = jnp.where(rowf == e1, NEG_INF, logits)
    m2 = jnp.max(rest, axis=0, keepdims=True)
    e2 = jnp.min(jnp.where(rest == m2, rowf, float(N_EXPERTS)), axis=0, keepdims=True)
    hot1, hot2 = rowf == e1, rowf == e2
    onehot = jnp.where(hot1 | hot2, 1.0, 0.0)
    tile_count = jnp.sum(onehot, axis=1, keepdims=True)

    @pl.when((phase == 0) & (i == 0))
    def _():
        count_ref[...] = jnp.zeros(count_ref.shape, F32)

    @pl.when(phase == 0)
    def _():
        count_ref[...] += tile_count

    @pl.when((phase == 1) & (i == 0))
    def _():
        padded = jnp.ceil(count_ref[...] / MOE_SLOTS) * MOE_SLOTS
        erow = lax.broadcasted_iota(jnp.int32, padded.shape, 0)
        start = jnp.zeros(padded.shape, F32)
        end = jnp.zeros(padded.shape, F32)
        for e in range(N_EXPERTS):
            pe = padded[e:e + 1, :]
            start = start + jnp.where(erow > e, pe, 0.0)
            end = end + jnp.where(erow >= e, pe, 0.0)
        base_ref[...] = start
        first_row = lax.broadcasted_iota(jnp.int32, (N_EXPERTS, blk_ref.shape[1]), 1).astype(F32) * MOE_SLOTS
        owner = jnp.sum(jnp.where(end[:, 0:1] <= first_row, 1.0, 0.0), axis=0, keepdims=True)
        used = jnp.max(end[:, 0:1], axis=0, keepdims=True) / MOE_SLOTS
        r8 = lax.broadcasted_iota(jnp.int32, blk_ref.shape, 0)
        blk = jnp.where(r8 == 0, jnp.minimum(owner, N_EXPERTS - 1.0), jnp.broadcast_to(used, blk_ref.shape))
        blk_ref[...] = blk.astype(jnp.int32)

    @pl.when(phase == 1)
    def _():
        upper = (lax.broadcasted_iota(jnp.int32, (tt, tt), 0)
                 < lax.broadcasted_iota(jnp.int32, (tt, tt), 1))
        before = _dot(onehot.astype(BF16), jnp.where(upper, 1.0, 0.0).astype(BF16))
        slot = base_ref[:, 0:1] + before
        d1 = jnp.sum(jnp.where(hot1, slot, 0.0), axis=0, keepdims=True)
        d2 = jnp.sum(jnp.where(hot2, slot, 0.0), axis=0, keepdims=True)
        r8 = lax.broadcasted_iota(jnp.int32, (8, tt), 0)
        dest_ref[0] = jnp.where(r8 == 0, d1, d2).astype(jnp.int32)
        z = jnp.exp(m2 - m1)
        w1 = 1.0 / (1.0 + z)
        wts_ref[0] = jnp.where(r8 == 0, w1, z * w1)
        base_ref[...] += tile_count


def _route(x, w_router_t):
    n = x.shape[0]
    nt = n // MOE_TOKENS
    n_blocks = _n_slot_blocks(n)
    blk_lanes = -(-n_blocks // LANES) * LANES
    tile = lambda ph, i: (i * ph, 0, 0)
    return pl.pallas_call(
        functools.partial(_route_kernel, n_blocks=n_blocks),
        grid=(2, nt),
        in_specs=[pl.BlockSpec((MOE_TOKENS, D_MODEL), lambda ph, i: (i, 0)),
                  pl.BlockSpec(w_router_t.shape, lambda ph, i: (0, 0))],
        out_specs=(pl.BlockSpec((1, 8, MOE_TOKENS), tile), pl.BlockSpec((1, 8, MOE_TOKENS), tile),
                   pl.BlockSpec((8, blk_lanes), lambda ph, i: (0, 0))),
        out_shape=(jax.ShapeDtypeStruct((nt, 8, MOE_TOKENS), jnp.int32),
                   jax.ShapeDtypeStruct((nt, 8, MOE_TOKENS), F32),
                   jax.ShapeDtypeStruct((8, blk_lanes), jnp.int32)),
        scratch_shapes=[pltpu.VMEM((N_EXPERTS, LANES), F32), pltpu.VMEM((N_EXPERTS, LANES), F32)],
        compiler_params=_cparams(("arbitrary", "arbitrary")),
        name="moe_route",
    )(x, w_router_t)


def _row_copy(src, src_row, dst, dst_row, sem):
    return pltpu.make_async_copy(src.at[pl.ds(src_row, 1), :], dst.at[pl.ds(dst_row, 1), :], sem)


def _scatter_kernel(d1_ref, d2_ref, x_ref, init_ref, xs_ref, sem):
    del init_ref
    i = pl.program_id(0)
    tt = x_ref.shape[0]

    def issue(r, carry):
        t = i * tt + r
        _row_copy(x_ref, r, xs_ref, d1_ref[t], sem).start()
        _row_copy(x_ref, r, xs_ref, d2_ref[t], sem).start()
        return carry

    lax.fori_loop(0, tt, issue, 0)

    def drain(r, carry):
        _row_copy(x_ref, 0, xs_ref, 0, sem).wait()
        return carry

    lax.fori_loop(0, 2 * tt, drain, 0)


def _scatter_rows(x, d1, d2, n_slots):
    n = x.shape[0]
    init = jnp.zeros((n_slots, D_MODEL), F32)
    return pl.pallas_call(
        _scatter_kernel,
        grid_spec=pltpu.PrefetchScalarGridSpec(
            num_scalar_prefetch=2,
            grid=(n // MOE_TOKENS,),
            in_specs=[pl.BlockSpec((MOE_TOKENS, D_MODEL), lambda i, a, b: (i, 0)),
                      pl.BlockSpec(memory_space=pl.ANY)],
            out_specs=pl.BlockSpec(memory_space=pl.ANY),
            scratch_shapes=[pltpu.SemaphoreType.DMA(())],
        ),
        out_shape=jax.ShapeDtypeStruct((n_slots, D_MODEL), F32),
        input_output_aliases={3: 0},
        compiler_params=_cparams(("arbitrary",)),
        name="moe_scatter",
    )(d1, d2, x, init)


FF_HALF = D_FF // 2


def _expert_ffn_kernel(blk_ref, x_ref, w1_ref, w2_ref, y_ref):
    j = pl.program_id(0)

    @pl.when(j < blk_ref[1, 0])
    def _():
        x = x_ref[...].astype(BF16)
        y = jnp.zeros(y_ref.shape, F32)
        for c in range(2):
            g = _dot(x, w1_ref[0, :, c * FF_HALF:(c + 1) * FF_HALF])
            u = _dot(x, w1_ref[0, :, D_FF + c * FF_HALF:D_FF + (c + 1) * FF_HALF])
            act = (jax.nn.silu(g) * u).astype(BF16)
            y = y + _dot(act, w2_ref[0, c * FF_HALF:(c + 1) * FF_HALF, :])
        y_ref[...] = y

    @pl.when(j >= blk_ref[1, 0])
    def _():
        y_ref[...] = jnp.zeros(y_ref.shape, F32)


def _expert_ffn(xs, blk, w1, w2):
    n_slots = xs.shape[0]
    return pl.pallas_call(
        _expert_ffn_kernel,
        grid_spec=pltpu.PrefetchScalarGridSpec(
            num_scalar_prefetch=1,
            grid=(n_slots // MOE_SLOTS,),
            in_specs=[pl.BlockSpec((MOE_SLOTS, D_MODEL), lambda j, b: (j, 0)),
                      pl.BlockSpec((1, D_MODEL, 2 * D_FF), lambda j, b: (b[0, j], 0, 0)),
                      pl.BlockSpec((1, D_FF, D_MODEL), lambda j, b: (b[0, j], 0, 0))],
            out_specs=pl.BlockSpec((MOE_SLOTS, D_MODEL), lambda j, b: (j, 0)),
        ),
        out_shape=jax.ShapeDtypeStruct((n_slots, D_MODEL), F32),
        compiler_params=_cparams(("arbitrary",)),
        name="moe_expert_ffn",
    )(blk, xs, w1, w2)


def _combine_kernel(d1_ref, d2_ref, ys_ref, x_ref, wts_ref, g_ref, b_ref, o_ref, buf_ref, sem, *, alpha):
    i = pl.program_id(0)
    tt = x_ref.shape[0]

    def issue(r, carry):
        t = i * tt + r
        _row_copy(ys_ref, d1_ref[t], buf_ref.at[0], r, sem).start()
        _row_copy(ys_ref, d2_ref[t], buf_ref.at[1], r, sem).start()
        return carry

    lax.fori_loop(0, tt, issue, 0)

    def drain(r, carry):
        _row_copy(ys_ref, 0, buf_ref.at[0], 0, sem).wait()
        return carry

    lax.fori_loop(0, 2 * tt, drain, 0)
    wcol = wts_ref[0].T
    y = wcol[:, 0:1] * buf_ref[0] + wcol[:, 1:2] * buf_ref[1]
    o_ref[...] = _ln(alpha * x_ref[...] + y, g_ref[...], b_ref[...])


def _combine(ys, x, d1, d2, wts, g, b, *, alpha):
    n = x.shape[0]
    row = pl.BlockSpec((MOE_TOKENS, D_MODEL), lambda i, p, q: (i, 0))
    vec = pl.BlockSpec((1, D_MODEL), lambda i, p, q: (0, 0))
    return pl.pallas_call(
        functools.partial(_combine_kernel, alpha=alpha),
        grid_spec=pltpu.PrefetchScalarGridSpec(
            num_scalar_prefetch=2,
            grid=(n // MOE_TOKENS,),
            in_specs=[pl.BlockSpec(memory_space=pl.ANY), row,
                      pl.BlockSpec((1, 8, MOE_TOKENS), lambda i, p, q: (i, 0, 0)), vec, vec],
            out_specs=row,
            scratch_shapes=[pltpu.VMEM((2, MOE_TOKENS, D_MODEL), F32), pltpu.SemaphoreType.DMA(())],
        ),
        out_shape=jax.ShapeDtypeStruct((n, D_MODEL), F32),
        compiler_params=_cparams(("arbitrary",)),
        name="moe_combine",
    )(d1, d2, ys, x, wts, g, b)


def _moe_ffn(x, w_router_t, w1, w2, g, b, *, alpha):
    n = x.shape[0]
    dest, wts, blk = _route(x, w_router_t)
    d1, d2 = dest[:, 0, :].reshape(n), dest[:, 1, :].reshape(n)
    xs = _scatter_rows(x, d1, d2, _n_slot_blocks(n) * MOE_SLOTS)
    ys = _expert_ffn(xs, blk, w1, w2)
    return _combine(ys, x, d1, d2, wts, g, b, alpha=alpha)


def _prep_layer(l, w_in, w_phi1, w_phi2, gmlp_ln_g, gmlp_ln_b, gmlp_ws, gmlp_bs, w_out, ln1_g, ln1_b):
    wi = w_in[l]
    c0 = NSA_WIDTH
    c1, c2, c3 = c0 + KV_COLS, c0 + 2 * KV_COLS, c0 + 3 * KV_COLS
    c4 = c3 + N_GATES
    c5 = c4 + GMLP_WIDTH
    eye = jnp.eye(N_KV_HEADS, dtype=F32)
    w1 = w_phi1[l]
    wa = jnp.einsum("cldf,kK,cC->lkcdKCf", w1[:, :CMP_STRIDE], eye, eye)
    wc = jnp.einsum("cldf,kK,cC->lkcdKCf", w1[:, CMP_STRIDE:], eye, eye)
    rows = CMP_STRIDE * KV_COLS
    phi1_rows = jnp.concatenate([wa.reshape(rows, KV_COLS), wc.reshape(rows, KV_COLS)], axis=1)
    phi2_big = jnp.einsum("cfe,kK,cC->kcfKCe", w_phi2[l], eye, eye).reshape(KV_COLS, KV_COLS)
    ws, bs = gmlp_ws[l], gmlp_bs[l]
    g_bias = jnp.repeat(bs.reshape(GMLP_WIDTH // LANES, 2, CHUNK).transpose(0, 2, 1), GMLP_GROUP_DIM, axis=2)
    return {
        "wq": (wi[:, :c0] * SCALE).astype(BF16),
        "wkvt": wi[:, c0:c3].T.astype(BF16),
        "wkvc": wi[:, c0:c1].astype(BF16),
        "wg": jnp.pad(wi[:, c3:c4], ((0, 0), (0, LANES - N_GATES))).astype(BF16),
        "wu": wi[:, c4:c5].astype(BF16),
        "wv": wi[:, c5:].astype(BF16),
        "lng": gmlp_ln_g[l][None, :],
        "lnb": gmlp_ln_b[l][None, :],
        "g_tril": jnp.tril(ws).astype(BF16),
        "g_bias": g_bias,
        "g_diag_w": jnp.repeat(ws[:, 0, 0], GMLP_GROUP_DIM)[None, :],
        "g_diag_b": jnp.repeat(bs[:, 0], GMLP_GROUP_DIM)[None, :],
        "phi1_rows": phi1_rows.astype(BF16),
        "phi2_big": phi2_big.astype(BF16),
        "w_phi1": w1,
        "wo_a": w_out[l][:NSA_WIDTH].astype(BF16),
        "wo_g": w_out[l][NSA_WIDTH:].astype(BF16),
        "ln1_g": ln1_g[l][None, :],
        "ln1_b": ln1_b[l][None, :],
    }


def _kv_leaf(kvt):
    return kvt.reshape(N_KV_HEADS, 2, HEAD_DIM, kvt.shape[-1]).transpose(3, 0, 1, 2)


def _mix_prompt(xp, w, alpha):
    q, kvt, kvtb, kvc_rows, gates, og, _ = _inproj(xp, w, sample=False, tm=256)
    cmp_kv = _compress_prompt(kvc_rows, w)
    o_cmp, unsel, flags = _cmp_select(q, cmp_kv, tq=128)
    o_attn = _slc_win(q, kvtb, unsel, flags, gates, o_cmp, tq=128)
    x1 = _outproj(o_attn, og, xp, w, alpha=alpha, tm=256, name="outproj_prompt")
    return x1, kvt


def _cache_view(cache):
    return cache.transpose(0, 1, 3, 4, 5, 2)


def _mix_sample(xs, w, alpha, cmp_t, slc_t, win_t, page_table, layer):
    b = xs.shape[0]
    t_past = page_table.shape[1] * PAGE_SIZE
    q, kvt, _, _, gates, og, vn = _inproj(xs, w, sample=True, tm=b)
    kvt_cmp = _finish_sample(_compress_sample(cmp_t, page_table, w, layer), w)
    q8 = q.astype(F32).reshape(b, N_HEADS, HEAD_DIM)
    kv_new = kvt.T.reshape(b, 3, N_KV_HEADS, 2, HEAD_DIM)
    per_head = lambda kv: jnp.repeat(kv.transpose(0, 2, 1, 3), GROUP, axis=2)
    slopes = jnp.broadcast_to(jnp.exp2(-jnp.arange(1, N_HEADS + 1, dtype=F32))[:, None], (N_HEADS, LANES))
    o_cmp, idx = _cmp_select_sample(q8, kvt_cmp, slopes, t_past=t_past)
    sel = idx[:, ::GROUP, :N_SEL_BLOCKS].reshape(b, N_KV_HEADS * N_SEL_BLOCKS)
    gates8 = gates[:, :N_GATES].reshape(b, 3, N_HEADS, 1)
    o_attn = _slc_win_sample(sel, page_table, q8, slc_t, win_t, per_head(kv_new[:, 1]), per_head(kv_new[:, 2]),
                             gates8, o_cmp, slopes, layer, t_past=t_past)
    o_attn = o_attn.reshape(b, NSA_WIDTH).astype(BF16)
    x1 = _outproj(o_attn, og, xs, w, alpha=alpha, tm=b, name="outproj_sample")
    return x1, kv_new, vn


def kernel(x_prompt, x_sample, cache_cmp, cache_slc, cache_win, page_table, w_in, w_phi1, w_phi2, gmlp_ln_g,
           gmlp_ln_b, gmlp_ws, gmlp_bs, w_out, ln1_g, ln1_b, ln2_g, ln2_b, ffn_w1, ffn_w2, moe_router,
           moe_w1, moe_w2):
    depth = w_in.shape[0]
    alpha = (2 * depth) ** 0.25
    t, b = x_prompt.shape[1], x_sample.shape[0]
    xp, xs = x_prompt[0], x_sample[:, 0]
    cmp_t, slc_t, win_t = _cache_view(cache_cmp), _cache_view(cache_slc), _cache_view(cache_win)
    n_win = min(WINDOW, t)
    outs = [[] for _ in range(7)]
    for l in range(depth):
        w = _prep_layer(l, w_in, w_phi1, w_phi2, gmlp_ln_g, gmlp_ln_b, gmlp_ws, gmlp_bs, w_out, ln1_g, ln1_b)
        x1p, kvt = _mix_prompt(xp, w, alpha)
        x1s, kv_new, vn = _mix_sample(xs, w, alpha, cmp_t, slc_t, win_t, page_table, l)
        outs[0].append(_kv_leaf(kvt[:KV_COLS])[None])
        outs[1].append(_kv_leaf(kvt[KV_COLS:2 * KV_COLS])[None])
        outs[2].append(_kv_leaf(kvt[2 * KV_COLS:, t - n_win:])[None])
        for k in range(3):
            outs[3 + k].append(kv_new[:, k][:, None])
        outs[6].append(vn[:, None, :])
        g2, b2 = ln2_g[l][None, :], ln2_b[l][None, :]
        i = l // 2
        if l % 2 == 0:
            w1, w2 = ffn_w1[i].astype(BF16), ffn_w2[i].astype(BF16)
            xp = _ffn_dense(x1p, w1, w2, g2, b2, alpha=alpha, tm=256, name="ffn_prompt")
            xs = _ffn_dense(x1s, w1, w2, g2, b2, alpha=alpha, tm=b, name="ffn_sample")
        else:
            pad = -(t + b) % MOE_TOKENS
            x_all = jnp.concatenate([x1p, x1s, jnp.zeros((pad, D_MODEL), F32)], axis=0)
            y_all = _moe_ffn(x_all, moe_router[i].T, moe_w1[i].astype(BF16), moe_w2[i].astype(BF16), g2, b2,
                             alpha=alpha)
            xp, xs = y_all[:t], y_all[t:t + b]
    return (xp[None], xs[:, None]) + tuple(jnp.stack(o) for o in outs)
```

```python
import functools

import jax
import jax.numpy as jnp
import numpy as np
from jax import lax
from jax.experimental import pallas as pl
from jax.experimental.pallas import tpu as pltpu

F32 = jnp.float32
BF16 = jnp.bfloat16

D_MODEL = 1024
N_HEADS = 8
N_KV_HEADS = 2
GROUP = N_HEADS // N_KV_HEADS
HEAD_DIM = 64
NSA_WIDTH = N_HEADS * HEAD_DIM
SCALE = HEAD_DIM ** -0.5
CMP_STRIDE = 16
CMP_LEN = 32
SLC_BLOCK = 64
N_SEL_BLOCKS = 16
WINDOW = 512
GMLP_GROUPS = 8
GMLP_GROUP_DIM = 64
GMLP_WIDTH = GMLP_GROUPS * GMLP_GROUP_DIM
CHUNK = 128
KV_COLS = N_KV_HEADS * 2 * HEAD_DIM
N_GATES = 3 * N_HEADS
D_FF = 2816
N_EXPERTS = 8
PAGE_SIZE = 128
LN_EPS = 1e-5

LANES = 128
VMEM_LIMIT = 56 * 1024 * 1024


def _cparams(sem, vmem=VMEM_LIMIT):
    return pltpu.CompilerParams(dimension_semantics=sem, vmem_limit_bytes=vmem)


def _ln(x, g, b):
    xc = x - jnp.mean(x, axis=-1, keepdims=True)
    var = jnp.mean(xc * xc, axis=-1, keepdims=True)
    return xc * lax.rsqrt(var + LN_EPS) * g + b


def _dot(a, b):
    return jnp.dot(a, b, preferred_element_type=F32)


def _dot_nt(a, b):
    return lax.dot_general(a, b, (((1,), (1,)), ((), ())), preferred_element_type=F32)


def _full(shape):
    n = len(shape)
    return pl.BlockSpec(shape, lambda *_: (0,) * n)


def _inproj_kernel(x_ref, wq_ref, wkvt_ref, wkvc_ref, wg_ref, wu_ref, wv_ref, lng_ref, lnb_ref,
                   gw_ref, gb_ref, q_ref, kvt_ref, kvtb_ref, kvc_ref, gates_ref, og_ref, vn_ref, *,
                   sample):
    x = x_ref[...].astype(BF16)
    q_ref[...] = _dot(x, wq_ref[...]).astype(q_ref.dtype)
    kvt = _dot_nt(wkvt_ref[...], x)
    kvt_ref[...] = kvt
    kvtb_ref[...] = kvt[KV_COLS:].astype(kvtb_ref.dtype)
    kvc_ref[...] = _dot(x, wkvc_ref[...]).astype(kvc_ref.dtype)
    gates_ref[...] = jax.nn.sigmoid(_dot(x, wg_ref[...]))
    u = jax.nn.gelu(_dot(x, wu_ref[...]))
    v = jax.nn.gelu(_dot(x, wv_ref[...]))
    vn = _ln(v, lng_ref[...], lnb_ref[...])
    vn_ref[...] = vn
    if sample:
        og_ref[...] = (u * (vn * gw_ref[...] + gb_ref[...])).astype(og_ref.dtype)
    else:
        tm = x.shape[0]
        lane = lax.broadcasted_iota(jnp.int32, (CHUNK, LANES), 1)
        first = lane < GMLP_GROUP_DIM
        vb = vn.astype(BF16)
        for c in range(tm // CHUNK):
            rows = slice(c * CHUNK, (c + 1) * CHUNK)
            for p in range(GMLP_WIDTH // LANES):
                cols = slice(p * LANES, (p + 1) * LANES)
                vp = vb[rows, cols]
                mix = jnp.where(first, _dot(gw_ref[2 * p], vp), _dot(gw_ref[2 * p + 1], vp))
                og_ref[rows, cols] = (u[rows, cols] * (mix + gb_ref[p])).astype(og_ref.dtype)


def _inproj(x, w, *, sample, tm):
    rows = x.shape[0]
    grid = (rows // tm,)
    row = lambda n: pl.BlockSpec((tm, n), lambda i: (i, 0))
    if sample:
        gw_spec, gb_spec = _full((1, GMLP_WIDTH)), _full((1, GMLP_WIDTH))
        gw, gb = w["g_diag_w"], w["g_diag_b"]
    else:
        gw_spec, gb_spec = _full((GMLP_GROUPS, CHUNK, CHUNK)), _full((GMLP_WIDTH // LANES, CHUNK, LANES))
        gw, gb = w["g_tril"], w["g_bias"]
    out_shape = (
        jax.ShapeDtypeStruct((rows, NSA_WIDTH), BF16),
        jax.ShapeDtypeStruct((3 * KV_COLS, rows), F32),
        jax.ShapeDtypeStruct((2 * KV_COLS, rows), BF16),
        jax.ShapeDtypeStruct((rows, KV_COLS), BF16),
        jax.ShapeDtypeStruct((rows, LANES), F32),
        jax.ShapeDtypeStruct((rows, GMLP_WIDTH), BF16),
        jax.ShapeDtypeStruct((rows, GMLP_WIDTH), F32),
    )
    return pl.pallas_call(
        functools.partial(_inproj_kernel, sample=sample),
        grid=grid,
        in_specs=[row(D_MODEL), _full((D_MODEL, NSA_WIDTH)), _full((3 * KV_COLS, D_MODEL)),
                  _full((D_MODEL, KV_COLS)), _full((D_MODEL, LANES)), _full((D_MODEL, GMLP_WIDTH)),
                  _full((D_MODEL, GMLP_WIDTH)), _full((1, GMLP_WIDTH)), _full((1, GMLP_WIDTH)),
                  gw_spec, gb_spec],
        out_specs=(row(NSA_WIDTH), pl.BlockSpec((3 * KV_COLS, tm), lambda i: (0, i)),
                   pl.BlockSpec((2 * KV_COLS, tm), lambda i: (0, i)), row(KV_COLS),
                   row(LANES), row(GMLP_WIDTH), row(GMLP_WIDTH)),
        out_shape=out_shape,
        compiler_params=_cparams(("parallel",)),
        name="inproj_sample" if sample else "inproj_prompt",
    )(x, w["wq"], w["wkvt"], w["wkvc"], w["wg"], w["wu"], w["wv"], w["lng"], w["lnb"], gw, gb)


def _outproj_kernel(oa_ref, og_ref, x_ref, wa_ref, wg_ref, g_ref, b_ref, *rest, alpha, n_tiles):
    o_ref = rest[-1]
    rows = x_ref.shape[0]

    @pl.when(pl.program_id(0) < n_tiles)
    def _():
        mixed = _dot(oa_ref[...], wa_ref[...]) + _dot(og_ref[...], wg_ref[...])
        o_ref[:rows, :] = _ln(alpha * x_ref[...] + mixed, g_ref[...], b_ref[...])
        if o_ref.shape[0] > rows:
            o_ref[rows:, :] = jnp.zeros((o_ref.shape[0] - rows, D_MODEL), F32)

    @pl.when(pl.program_id(0) >= n_tiles)
    def _():
        o_ref[...] = jnp.zeros(o_ref.shape, F32)


def _outproj(oa, og, x, w, *, alpha, tm, name, out_rows=None, append_to=None):
    rows = x.shape[0]
    n_tiles = rows // tm
    row = lambda n: pl.BlockSpec((tm, n), lambda i: (jnp.minimum(i, n_tiles - 1), 0))
    in_specs = [row(NSA_WIDTH), row(GMLP_WIDTH), row(D_MODEL), _full((NSA_WIDTH, D_MODEL)),
                _full((GMLP_WIDTH, D_MODEL)), _full((1, D_MODEL)), _full((1, D_MODEL))]
    args = [oa, og, x, w["wo_a"], w["wo_g"], w["ln1_g"], w["ln1_b"]]
    grid = (n_tiles,)
    if append_to is None:
        out_spec = pl.BlockSpec((tm, D_MODEL), lambda i: (i, 0))
        out_shape = jax.ShapeDtypeStruct((out_rows or rows, D_MODEL), F32)
        grid = ((out_rows or rows) // tm,)
        aliases = {}
    else:
        start_tile = (append_to.shape[0] - MOE_TOKENS) // MOE_TOKENS
        in_specs.append(pl.BlockSpec(memory_space=pl.ANY))
        args.append(append_to)
        out_spec = pl.BlockSpec((MOE_TOKENS, D_MODEL), lambda i: (start_tile, 0))
        out_shape = jax.ShapeDtypeStruct(append_to.shape, F32)
        aliases = {len(args) - 1: 0}
    return pl.pallas_call(
        functools.partial(_outproj_kernel, alpha=alpha, n_tiles=n_tiles),
        grid=grid,
        in_specs=in_specs,
        out_specs=out_spec,
        out_shape=out_shape,
        input_output_aliases=aliases,
        compiler_params=_cparams(("parallel",)),
        name=name,
    )(*args)


def _ffn_kernel(x_ref, w1_ref, w2_ref, g_ref, b_ref, o_ref, *, alpha):
    x = x_ref[...]
    h = _dot(x.astype(BF16), w1_ref[...])
    act = (jax.nn.silu(h[:, :D_FF]) * h[:, D_FF:]).astype(BF16)
    y = _dot(act, w2_ref[...])
    o_ref[...] = _ln(alpha * x + y, g_ref[...], b_ref[...])


def _ffn_dense(x, w1, w2, g, b, *, alpha, tm, name):
    rows = x.shape[0]
    row = lambda n: pl.BlockSpec((tm, n), lambda i: (i, 0))
    return pl.pallas_call(
        functools.partial(_ffn_kernel, alpha=alpha),
        grid=(rows // tm,),
        in_specs=[row(D_MODEL), _full((D_MODEL, 2 * D_FF)), _full((D_FF, D_MODEL)),
                  _full((1, D_MODEL)), _full((1, D_MODEL))],
        out_specs=row(D_MODEL),
        out_shape=jax.ShapeDtypeStruct((rows, D_MODEL), F32),
        compiler_params=_cparams(("parallel",)),
        name=name,
    )(x, w1, w2, g, b)


def _finish_compress(ac, w2):
    n = ac.shape[0]
    a, c = ac[:, :KV_COLS], ac[:, KV_COLS:]
    row = lax.broadcasted_iota(jnp.int32, (n, KV_COLS), 0)
    c_next = jnp.where(row == n - 1, 0.0, pltpu.roll(c, n - 1, 0))
    h = jax.nn.gelu(a + c_next)
    return _dot(h.astype(BF16), w2)


def _compress_prompt_kernel(xs_ref, w1_ref, w2_ref, o_ref):
    o_ref[...] = _finish_compress(_dot(xs_ref[...], w1_ref[...]), w2_ref[...])


def _compress_prompt(kvc_rows, w):
    t = kvc_rows.shape[0]
    n_sub = t // CMP_STRIDE
    xs = kvc_rows.reshape(n_sub, CMP_STRIDE * KV_COLS)
    return pl.pallas_call(
        _compress_prompt_kernel,
        grid=(1,),
        in_specs=[_full(xs.shape), _full(w["phi1_rows"].shape), _full(w["phi2_big"].shape)],
        out_specs=_full((n_sub, KV_COLS)),
        out_shape=jax.ShapeDtypeStruct((n_sub, KV_COLS), F32),
        compiler_params=_cparams(("arbitrary",)),
        name="compress_prompt",
    )(xs, w["phi1_rows"], w["phi2_big"])


NEG_INF = float("-inf")


def _slope(head):
    return 2.0 ** -(head + 1)


def _cmp_probs(s, dist, valid, slope):
    s = jnp.where(valid, s - slope * dist, NEG_INF)
    m = jnp.max(s, axis=-1, keepdims=True)
    m = jnp.where(jnp.isfinite(m), m, 0.0)
    e = jnp.exp(s - m)
    return e / jnp.maximum(jnp.sum(e, axis=-1, keepdims=True), 1e-30)


def _select_blocks(pooled, cur, n_sel):
    blk = lax.broadcasted_iota(jnp.int32, pooled.shape, 1)
    blkf = blk.astype(F32)
    forced = (blk == 0) | (blk == cur) | (blk == cur - 1)
    work = jnp.where(blk <= cur, jnp.where(forced, jnp.inf, pooled), NEG_INF)
    sel = jnp.zeros(pooled.shape, F32)
    picks = []
    for _ in range(min(N_SEL_BLOCKS, n_sel)):
        m = jnp.max(work, axis=-1, keepdims=True)
        idx = jnp.min(jnp.where(work == m, blkf, 1e9), axis=-1, keepdims=True)
        pick = blkf == idx
        sel = jnp.maximum(sel, jnp.where(pick, jnp.where(m > NEG_INF, 1.0, 0.0), 0.0))
        work = jnp.where(pick, NEG_INF, work)
        picks.append(idx)
    return sel, picks


def _select_mask(pooled, cur, tri):
    n_top = min(N_SEL_BLOCKS, pooled.shape[1])
    blk = lax.broadcasted_iota(jnp.int32, pooled.shape, 1)
    forced = (blk == 0) | (blk == cur) | (blk == cur - 1)
    work = jnp.where(blk <= cur, jnp.where(forced, jnp.inf, pooled), NEG_INF)
    never = float(n_top)
    rnd = jnp.full(pooled.shape, never + 1.0, F32)
    counts = []
    for r in range(n_top):
        m = jnp.max(work, axis=-1, keepdims=True)
        pick = work == m
        live = m > NEG_INF
        rnd = jnp.where(pick, jnp.where(live, float(r), rnd), rnd)
        counts.append(jnp.sum(jnp.where(pick, jnp.where(live, 1.0, 0.0), 0.0), axis=-1, keepdims=True))
        work = jnp.where(pick, NEG_INF, work)
    taken = jnp.zeros((pooled.shape[0], 1), F32)
    last = jnp.full((pooled.shape[0], 1), never, F32)
    taken_before_last = jnp.zeros((pooled.shape[0], 1), F32)
    for r in range(n_top):
        crosses = (taken < n_top) & (taken + counts[r] >= n_top)
        last = jnp.where(crosses, float(r), last)
        taken_before_last = jnp.where(crosses, taken, taken_before_last)
        taken = taken + counts[r]
    in_last = jnp.where(rnd == last, 1.0, 0.0)
    lower_ties = _dot(in_last.astype(BF16), tri)
    return jnp.where(rnd < last, 1.0, in_last * jnp.where(taken_before_last + lower_ties < n_top, 1.0, 0.0))


def _cmp_select_kernel(q_ref, kbd_ref, vl_ref, vr_ref, cpos_ref, tri_ref, oc_ref, unsel_ref, flag_ref, *,
                       n_sel, tq):
    n_cmp = 4 * n_sel
    q0 = pl.program_id(0) * tq
    qpos = q0 + lax.broadcasted_iota(jnp.int32, (tq, n_cmp), 0)
    dist = qpos.astype(F32) - cpos_ref[...]
    valid = dist >= 0
    cur = lax.shift_right_logical(q0 + lax.broadcasted_iota(jnp.int32, (tq, n_sel), 0), 6)
    for kvh in range(N_KV_HEADS):
        imp = jnp.zeros((tq, n_cmp), F32)
        for p in range(GROUP // 2):
            cols = slice((kvh * 2 + p) * LANES, (kvh * 2 + p + 1) * LANES)
            s2 = _dot(q_ref[:, cols], kbd_ref[kvh])
            o_pair = jnp.zeros((tq, LANES), F32)
            for j, v_ref in enumerate((vl_ref, vr_ref)):
                head = kvh * GROUP + 2 * p + j
                pr = _cmp_probs(s2[:, j * n_cmp:(j + 1) * n_cmp], dist, valid, _slope(head))
                imp = imp + pr
                o_pair = o_pair + _dot(pr.astype(BF16), v_ref[kvh])
            oc_ref[:, cols] = o_pair
        pooled = imp[:, :n_sel]
        for r in range(1, 4):
            pooled = pooled + imp[:, r * n_sel:(r + 1) * n_sel]
        sel = _select_mask(pooled, cur, tri_ref[...])
        unsel_ref[kvh] = (1.0 - sel).T
        flag_ref[0, kvh:kvh + 1, :] = jnp.max(sel, axis=0, keepdims=True).astype(jnp.int32)


def _cmp_select(q, cmp_kv, *, tq):
    t = q.shape[0]
    n_cmp = cmp_kv.shape[0]
    n_sel = n_cmp // 4
    perm = cmp_kv.reshape(n_sel, 4, N_KV_HEADS, 2, HEAD_DIM).transpose(2, 3, 1, 0, 4)
    perm = perm.reshape(N_KV_HEADS, 2, n_cmp, HEAD_DIM)
    kc, vc = perm[:, 0], perm[:, 1]
    kct = jnp.swapaxes(kc, 1, 2).astype(BF16)
    z = jnp.zeros_like(kct)
    kbd = jnp.concatenate([jnp.concatenate([kct, z], axis=2), jnp.concatenate([z, kct], axis=2)], axis=1)
    zv = jnp.zeros_like(vc)
    vl = jnp.concatenate([vc, zv], axis=2).astype(BF16)
    vr = jnp.concatenate([zv, vc], axis=2).astype(BF16)
    col = jnp.arange(n_cmp)
    cpos = ((4 * (col % n_sel) + col // n_sel) * CMP_STRIDE + (CMP_LEN - 1)).astype(F32)[None, :]
    tri = jnp.asarray(np.triu(np.ones((n_sel, n_sel), np.float32), 1), BF16)
    row = lambda n: pl.BlockSpec((tq, n), lambda i: (i, 0))
    return pl.pallas_call(
        functools.partial(_cmp_select_kernel, n_sel=n_sel, tq=tq),
        grid=(t // tq,),
        in_specs=[row(NSA_WIDTH), _full(kbd.shape), _full(vl.shape), _full(vr.shape), _full(cpos.shape),
                  _full(tri.shape)],
        out_specs=(row(NSA_WIDTH), pl.BlockSpec((N_KV_HEADS, n_sel, tq), lambda i: (0, 0, i)),
                   pl.BlockSpec((1, N_KV_HEADS, n_sel), lambda i: (i, 0, 0))),
        out_shape=(jax.ShapeDtypeStruct((t, NSA_WIDTH), F32),
                   jax.ShapeDtypeStruct((N_KV_HEADS, n_sel, t), F32),
                   jax.ShapeDtypeStruct((t // tq, N_KV_HEADS, n_sel), jnp.int32)),
        compiler_params=_cparams(("parallel",)),
        name="cmp_select_prompt",
    )(q, kbd, vl, vr, cpos, tri)


MASKED = -1e30


MASK_BIG = 2.0 ** 100
M_FLOOR = -(2.0 ** 99)
KIND_PLAIN, KIND_CAUSAL, KIND_EDGE, KIND_DUMMY = 0, 1, 2, 3
N_FEAT = 16
SLOTS = 3


def _unit_code(kind, chunk):
    return kind * 256 + chunk


def _bias_rows():
    lane = np.arange(2 * LANES)
    key = lane % LANES
    out = np.zeros((N_KV_HEADS, 4, GROUP // 2, N_FEAT, 2 * LANES), np.float32)
    for kvh in range(N_KV_HEADS):
        for pair in range(GROUP // 2):
            slope = 2.0 ** -(kvh * GROUP + 2 * pair + lane // LANES + 1.0)
            for cm in range(4):
                for f in range(8):
                    out[kvh, cm, pair, f] = np.where(f == 2 * cm + key // SLC_BLOCK, -MASK_BIG, 0.0)
                out[kvh, cm, pair, 8] = -slope
                out[kvh, cm, pair, 9] = slope * key
                out[kvh, cm, pair, 10] = slope * LANES
    return jnp.asarray(out, BF16)


def _gate_expansion():
    out = np.zeros((LANES, 3 * NSA_WIDTH), np.float32)
    col = np.arange(3 * NSA_WIDTH)
    out[col // HEAD_DIM, col] = 1.0
    return jnp.asarray(out, BF16)


def _unit_scores(code, i, kvh, branch, q_ref, kv_ref, unselt_ref, rhsx_ref, mb_ref, const8, rp_ref, vw_ref,
                 xt_ref):
    kind = lax.shift_right_logical(code, 8)
    c = code & 255
    lanes = pl.ds(pl.multiple_of(c * LANES, LANES), LANES)
    row0 = branch * KV_COLS + kvh * 2 * HEAD_DIM
    kt = kv_ref[row0:row0 + HEAD_DIM, lanes]
    vt = kv_ref[row0 + HEAD_DIM:row0 + 2 * HEAD_DIM, lanes]
    vw_ref[:HEAD_DIM, :LANES] = vt
    vw_ref[LANES + HEAD_DIM:, LANES:] = vt
    dummy = jnp.where(kind == KIND_DUMMY, 1.0, 0.0)
    if branch == 0:
        blocks8 = pl.ds(pl.multiple_of(lax.shift_right_logical(c, 2) * 8, 8), 8)
        xt_ref[0:8, :] = unselt_ref[kvh, blocks8, :] * (1.0 - dummy) + dummy
    else:
        xt_ref[0:8, :] = jnp.zeros((8, xt_ref.shape[1]), F32) + dummy
    row8 = lax.broadcasted_iota(jnp.int32, const8.shape, 0)
    xt_ref[8:16, :] = const8 + jnp.where(row8 == 2, (c - i).astype(F32), 0.0)
    x = xt_ref[...].T.astype(BF16)
    scores = []
    for pair in range(GROUP // 2):
        rp_ref[pair, :HEAD_DIM, :LANES] = kt
        rp_ref[pair, HEAD_DIM:LANES, LANES:] = kt
        rp_ref[pair, LANES:LANES + N_FEAT, :] = rhsx_ref[kvh, c & 3, pair]
        cols = slice((kvh * 2 + pair) * LANES, (kvh * 2 + pair + 1) * LANES)
        scores.append(_dot(jnp.concatenate([q_ref[:, cols], x], axis=1), rp_ref[pair]))
    return scores, mb_ref[jnp.where(kind == KIND_DUMMY, 0, kind)]


def _unit_probs(scores, mb, m_ref):
    out = []
    for pair, s2 in enumerate(scores):
        ps, alphas = [], []
        for j in range(2):
            h = 2 * pair + j
            s = s2[:, j * LANES:(j + 1) * LANES] + mb
            m_old = m_ref[h]
            m_new = jnp.maximum(m_old, jnp.max(s, axis=-1, keepdims=True))
            ps.append(jnp.exp(s - m_new).astype(BF16))
            alphas.append(jnp.exp(m_old - m_new))
            m_ref[h] = m_new
        out.append((jnp.concatenate(ps, axis=1), jnp.concatenate(alphas, axis=1)))
    return out


def _unit_accumulate(probs, vw_ref, acc_ref):
    for pair, (p2, alpha2) in enumerate(probs):
        acc_ref[pair] = alpha2 * acc_ref[pair] + _dot_nt(p2, vw_ref[...])


def _build_unit_lists(kvh, i, flag_ref, list_ref):
    a, b, w = kvh * SLOTS, kvh * SLOTS + 1, kvh * SLOTS + 2
    n_win = WINDOW // LANES
    list_ref[a, 0] = _unit_code(KIND_CAUSAL, i)
    list_ref[w, 0] = _unit_code(KIND_CAUSAL, i)
    n_w = jnp.int32(1)
    for d in range(1, n_win + 1):
        list_ref[w, n_w] = _unit_code(KIND_EDGE if d == n_win else KIND_PLAIN, jnp.maximum(i - d, 0))
        n_w = n_w + (i >= d).astype(jnp.int32)

    def scan(c, counts):
        n_a, n_b = counts
        code = _unit_code(KIND_PLAIN, c)
        list_ref[a, n_a] = code
        list_ref[b, n_b] = code
        picked = ((flag_ref[0, kvh, 2 * c] + flag_ref[0, kvh, 2 * c + 1]) > 0).astype(jnp.int32)
        to_a = (n_a <= n_b).astype(jnp.int32)
        return n_a + picked * to_a, n_b + picked * (1 - to_a)

    n_a, n_b = lax.fori_loop(0, i, scan, (jnp.int32(1), jnp.int32(0)))
    return n_a, n_b, n_w


def _attend_result(acc, lane):
    l_even = jnp.maximum(acc[:, HEAD_DIM:HEAD_DIM + 1], 1e-30)
    l_odd = jnp.maximum(acc[:, LANES:LANES + 1], 1e-30)
    return jnp.where(lane < HEAD_DIM, acc[:, :LANES] / l_even, acc[:, LANES:] / l_odd)


def _slc_win_kernel(flag_ref, q_ref, kv_ref, unselt_ref, rhsx_ref, egate_ref, gates_ref, oc_ref, o_ref,
                    list_ref, rp_ref, vw_ref, xt_ref, mb_ref, m_ref, acc_ref, *, tq):
    i = pl.program_id(0)
    n_slots = N_KV_HEADS * SLOTS
    row = lax.broadcasted_iota(jnp.int32, (tq, LANES), 0)
    lane = lax.broadcasted_iota(jnp.int32, (tq, LANES), 1)

    @pl.when(i == 0)
    def _():
        mb_ref[KIND_PLAIN] = jnp.zeros((tq, LANES), F32)
        mb_ref[KIND_CAUSAL] = jnp.where(lane <= row, 0.0, -MASK_BIG)
        mb_ref[KIND_EDGE] = jnp.where(lane >= row, 0.0, -MASK_BIG)
        rp_ref[...] = jnp.zeros(rp_ref.shape, rp_ref.dtype)
        vrow = lax.broadcasted_iota(jnp.int32, (2 * LANES, 2 * LANES), 0)
        vcol = lax.broadcasted_iota(jnp.int32, (2 * LANES, 2 * LANES), 1)
        ones_rows = ((vrow >= HEAD_DIM) & (vrow < LANES + HEAD_DIM)) & ((vrow < LANES) == (vcol < LANES))
        for s in range(n_slots):
            vw_ref[s] = jnp.where(ones_rows, 1.0, 0.0).astype(vw_ref.dtype)
        xt_ref[...] = jnp.zeros(xt_ref.shape, F32)

    m_ref[...] = jnp.full(m_ref.shape, M_FLOOR, F32)
    acc_ref[...] = jnp.zeros(acc_ref.shape, F32)
    row8 = lax.broadcasted_iota(jnp.int32, (8, tq), 0)
    q_index = lax.broadcasted_iota(jnp.int32, (8, tq), 1).astype(F32)
    const8 = jnp.where(row8 == 0, q_index, jnp.where(row8 == 1, 1.0, 0.0))

    counts = []
    for kvh in range(N_KV_HEADS):
        counts.extend(_build_unit_lists(kvh, i, flag_ref, list_ref))
    n_iter = functools.reduce(jnp.maximum, counts)
    for s in range(n_slots):
        def pad(t, carry, s=s):
            list_ref[s, t] = _unit_code(KIND_DUMMY, 0)
            return carry
        lax.fori_loop(counts[s], n_iter, pad, 0)

    def body(t, carry):
        for kvh in range(N_KV_HEADS):
            slots = range(kvh * SLOTS, (kvh + 1) * SLOTS)
            scored = [_unit_scores(list_ref[s, t], i, kvh, int(s % SLOTS == SLOTS - 1), q_ref, kv_ref, unselt_ref,
                                   rhsx_ref, mb_ref, const8, rp_ref.at[s], vw_ref.at[s], xt_ref.at[s])
                      for s in slots]
            probs = [_unit_probs(scores, mb, m_ref.at[s]) for s, (scores, mb) in zip(slots, scored)]
            for s, pr in zip(slots, probs):
                _unit_accumulate(pr, vw_ref.at[s], acc_ref.at[s])
        return carry

    lax.fori_loop(0, n_iter, body, 0)

    gates = gates_ref[...]
    g_hi = gates.astype(BF16)
    g_lo = (gates - g_hi.astype(F32)).astype(BF16)
    g = _dot(g_hi, egate_ref[...]) + _dot(g_lo, egate_ref[...])
    for kvh in range(N_KV_HEADS):
        a, b, w = kvh * SLOTS, kvh * SLOTS + 1, kvh * SLOTS + 2
        for pair in range(GROUP // 2):
            m_a = jnp.concatenate([m_ref[a, 2 * pair], m_ref[a, 2 * pair + 1]], axis=1)
            m_b = jnp.concatenate([m_ref[b, 2 * pair], m_ref[b, 2 * pair + 1]], axis=1)
            m = jnp.maximum(m_a, m_b)
            slc = jnp.exp(m_a - m) * acc_ref[a, pair] + jnp.exp(m_b - m) * acc_ref[b, pair]
            c0 = (kvh * 2 + pair) * LANES
            o_ref[:, c0:c0 + LANES] = (
                g[:, c0:c0 + LANES] * oc_ref[:, c0:c0 + LANES]
                + g[:, NSA_WIDTH + c0:NSA_WIDTH + c0 + LANES] * _attend_result(slc, lane)
                + g[:, 2 * NSA_WIDTH + c0:2 * NSA_WIDTH + c0 + LANES] * _attend_result(acc_ref[w, pair], lane)
            ).astype(o_ref.dtype)


def _slc_win(q, kvtb, unselt, flags, gates, o_cmp, *, tq):
    t = q.shape[0]
    n_sel = unselt.shape[1]
    rhsx, egate = _bias_rows(), _gate_expansion()
    n_slots = N_KV_HEADS * SLOTS
    max_units = 8 + t // LANES
    row = lambda n: pl.BlockSpec((tq, n), lambda i: (i, 0))
    return pl.pallas_call(
        functools.partial(_slc_win_kernel, tq=tq),
        grid=(t // tq,),
        in_specs=[pl.BlockSpec((1, N_KV_HEADS, n_sel), lambda i: (i, 0, 0), memory_space=pltpu.SMEM),
                  row(NSA_WIDTH), _full(kvtb.shape),
                  pl.BlockSpec((N_KV_HEADS, n_sel, tq), lambda i: (0, 0, i)), _full(rhsx.shape),
                  _full(egate.shape), row(LANES), row(NSA_WIDTH)],
        out_specs=row(NSA_WIDTH),
        out_shape=jax.ShapeDtypeStruct((t, NSA_WIDTH), BF16),
        scratch_shapes=[pltpu.SMEM((n_slots, max_units), jnp.int32),
                        pltpu.VMEM((n_slots, GROUP // 2, 2 * LANES, 2 * LANES), BF16),
                        pltpu.VMEM((n_slots, 2 * LANES, 2 * LANES), BF16),
                        pltpu.VMEM((n_slots, LANES, tq), F32),
                        pltpu.VMEM((3, tq, LANES), F32),
                        pltpu.VMEM((n_slots, GROUP, tq, LANES), F32),
                        pltpu.VMEM((n_slots, GROUP // 2, tq, 2 * LANES), F32)],
        compiler_params=_cparams(("arbitrary",)),
        name="slc_win_prompt",
    )(flags, q, kvtb, unselt, rhsx, egate, gates, o_cmp)


PAGES_PER_STEP = 32


def _compress_sample_kernel(pt_ref, *refs, n_pages):
    page_refs, (perm_ref, w1_ref, ac_ref, xs_ref) = refs[:n_pages], refs[n_pages:]
    per_page = PAGE_SIZE // CMP_STRIDE
    for p in range(n_pages):
        page_t = page_refs[p][0, 0].reshape(KV_COLS, PAGE_SIZE).astype(BF16)
        xs_ref[p] = _dot_nt(perm_ref[...], page_t)
    lhs = jnp.concatenate(
        [xs_ref[:, l * per_page:(l + 1) * per_page, :].reshape(n_pages * per_page, KV_COLS).astype(BF16)
         for l in range(CMP_STRIDE)], axis=1)
    ac_ref[0] = _dot(lhs, w1_ref[...])


def _compress_sample(cache_t, page_table, w, layer):
    b, n_pg = page_table.shape
    p = min(PAGES_PER_STEP, n_pg)
    n = p * PAGE_SIZE // CMP_STRIDE
    w1 = w["phi1_rows"]
    rows = np.arange(PAGE_SIZE)
    perm = np.zeros((PAGE_SIZE, PAGE_SIZE), np.float32)
    perm[rows, (rows % (PAGE_SIZE // CMP_STRIDE)) * CMP_STRIDE + rows // (PAGE_SIZE // CMP_STRIDE)] = 1.0
    perm = jnp.asarray(perm, BF16)

    def page_spec(j):
        return pl.BlockSpec((1, 1, N_KV_HEADS, 2, HEAD_DIM, PAGE_SIZE),
                            lambda bi, c, pt: (layer, pt[bi, c * p + j], 0, 0, 0, 0))

    return pl.pallas_call(
        functools.partial(_compress_sample_kernel, n_pages=p),
        grid_spec=pltpu.PrefetchScalarGridSpec(
            num_scalar_prefetch=1,
            grid=(b, n_pg // p),
            in_specs=[page_spec(j) for j in range(p)]
            + [pl.BlockSpec(perm.shape, lambda bi, c, pt: (0, 0)),
               pl.BlockSpec(w1.shape, lambda bi, c, pt: (0, 0))],
            out_specs=pl.BlockSpec((1, n, 2 * KV_COLS), lambda bi, c, pt: (bi, c, 0)),
            scratch_shapes=[pltpu.VMEM((p, PAGE_SIZE, KV_COLS), F32)],
        ),
        out_shape=jax.ShapeDtypeStruct((b, n_pg * PAGE_SIZE // CMP_STRIDE, 2 * KV_COLS), F32),
        compiler_params=_cparams(("parallel", "parallel")),
        name="compress_sample",
    )(page_table, *([cache_t] * p), perm, w1)


def _finish_sample_kernel(ac_ref, w2_ref, kvt_ref, res_ref):
    res = _finish_compress(ac_ref[0], w2_ref[...])
    n_sel = res.shape[0] // 4
    for kvh in range(N_KV_HEADS):
        res_ref[kvh] = res[:, kvh * LANES:(kvh + 1) * LANES]
        for r in range(4):
            kv = res_ref[kvh, pl.ds(r, n_sel, stride=4), :]
            kvt_ref[0, kvh, :, r * n_sel:(r + 1) * n_sel] = kv.T.astype(kvt_ref.dtype)


def _finish_sample(ac, w):
    b, n_cmp, _ = ac.shape
    return pl.pallas_call(
        _finish_sample_kernel,
        grid=(b,),
        in_specs=[pl.BlockSpec((1, n_cmp, 2 * KV_COLS), lambda i: (i, 0, 0)), _full((KV_COLS, KV_COLS))],
        out_specs=pl.BlockSpec((1, N_KV_HEADS, LANES, n_cmp), lambda i: (i, 0, 0, 0)),
        out_shape=jax.ShapeDtypeStruct((b, N_KV_HEADS, LANES, n_cmp), BF16),
        scratch_shapes=[pltpu.VMEM((N_KV_HEADS, n_cmp, LANES), F32)],
        compiler_params=_cparams(("parallel",)),
        name="finish_compress_sample",
    )(ac, w["phi2_big"])


SEQS_PER_STEP = 4


def _cmp_select_sample_kernel(q_ref, kvt_ref, cpos_ref, slope_ref, oc_ref, idx_ref, *, n_sel, t_past):
    n_cmp = 4 * n_sel
    n_seq = q_ref.shape[0]
    rows = n_seq * N_HEADS
    group0 = lax.broadcasted_iota(jnp.int32, (N_HEADS, n_cmp), 0) < GROUP
    s = []
    for b in range(n_seq):
        q = q_ref[b].astype(BF16)
        s.append(jnp.where(group0, _dot(q, kvt_ref[b, 0, :HEAD_DIM, :]), _dot(q, kvt_ref[b, 1, :HEAD_DIM, :])))
    s = jnp.concatenate(s, axis=0)
    dist = jnp.broadcast_to(float(t_past) - cpos_ref[...], (rows, n_cmp))
    pr = _cmp_probs(s, dist, dist >= 0, slope_ref[:, 0:1])
    prb = pr.astype(BF16)
    imp = []
    for b in range(n_seq):
        rb = slice(b * N_HEADS, (b + 1) * N_HEADS)
        oc_ref[b] = jnp.where(group0[:, :HEAD_DIM], _dot_nt(prb[rb], kvt_ref[b, 0, HEAD_DIM:, :]),
                              _dot_nt(prb[rb], kvt_ref[b, 1, HEAD_DIM:, :]))
        imp.append(jnp.where(group0, jnp.sum(pr[rb][:GROUP], axis=0, keepdims=True),
                             jnp.sum(pr[rb][GROUP:], axis=0, keepdims=True)))
    imp = jnp.concatenate(imp, axis=0)
    pooled = imp[:, :n_sel]
    for r in range(1, 4):
        pooled = pooled + imp[:, r * n_sel:(r + 1) * n_sel]
    pooled = jnp.concatenate([pooled, jnp.zeros((rows, LANES), F32)], axis=1)
    cur = jnp.full(pooled.shape, t_past // SLC_BLOCK, jnp.int32)
    _, picks = _select_blocks(pooled, cur, pooled.shape[1])
    lane = lax.broadcasted_iota(jnp.int32, (rows, LANES), 1)
    idx = jnp.zeros((rows, LANES), F32)
    for it, pick in enumerate(picks):
        idx = jnp.where(lane == it, pick, idx)
    idx_ref[...] = idx.astype(jnp.int32).reshape(idx_ref.shape)


def _cmp_select_sample(q8, kvt_cmp, slopes, *, t_past):
    b = q8.shape[0]
    g = SEQS_PER_STEP
    n_cmp = kvt_cmp.shape[-1]
    n_sel = n_cmp // 4
    col = jnp.arange(n_cmp)
    cpos = ((4 * (col % n_sel) + col // n_sel) * CMP_STRIDE + (CMP_LEN - 1)).astype(F32)[None, :]
    slopes = jnp.tile(slopes, (g, 1))
    return pl.pallas_call(
        functools.partial(_cmp_select_sample_kernel, n_sel=n_sel, t_past=t_past),
        grid=(b // g,),
        in_specs=[pl.BlockSpec((g, N_HEADS, HEAD_DIM), lambda i: (i, 0, 0)),
                  pl.BlockSpec((g, N_KV_HEADS, LANES, n_cmp), lambda i: (i, 0, 0, 0)),
                  _full(cpos.shape), _full(slopes.shape)],
        out_specs=(pl.BlockSpec((g, N_HEADS, HEAD_DIM), lambda i: (i, 0, 0)),
                   pl.BlockSpec((g, N_HEADS, LANES), lambda i: (i, 0, 0))),
        out_shape=(jax.ShapeDtypeStruct((b, N_HEADS, HEAD_DIM), F32),
                   jax.ShapeDtypeStruct((b, N_HEADS, LANES), jnp.int32)),
        compiler_params=_cparams(("parallel",)),
        name="cmp_select_sample",
    )(q8, kvt_cmp, cpos, slopes)


def _attend_pieces(pieces, q, k_rows, v_rows):
    s_new = jnp.sum(q * k_rows, axis=-1, keepdims=True)
    masked = [jnp.where(valid, s, MASKED) for s, valid, _ in pieces]
    m = s_new
    for s in masked:
        m = jnp.maximum(m, jnp.max(s, axis=-1, keepdims=True))
    p_new = jnp.exp(s_new - m)
    l, acc = p_new, p_new * v_rows
    for s, (_, valid, vt) in zip(masked, pieces):
        p = jnp.where(valid, jnp.exp(s - m), 0.0)
        l = l + jnp.sum(p, axis=-1, keepdims=True)
        acc = acc + _dot_nt(p.astype(BF16), vt)
    return acc / jnp.maximum(l, 1e-30)


def _slc_win_sample_kernel(sel_ref, pt_ref, q_ref, *refs, n_top, t_past):
    n_pages = N_KV_HEADS * n_top
    page_refs = refs[:n_pages]
    win_ref, snew_ref, wnew_ref, gates_ref, oc_ref, slope_ref, o_ref = refs[n_pages:]
    b = pl.program_id(0)
    q = q_ref[0]
    qb = q.astype(BF16)
    slope = slope_ref[:, 0:1]
    row_kvh = lax.shift_right_logical(lax.broadcasted_iota(jnp.int32, (N_HEADS, LANES), 0), 2)
    lane = lax.broadcasted_iota(jnp.int32, (N_HEADS, LANES), 1)
    lane_half = lax.shift_right_logical(lane, 6)
    pieces = []
    for kvh in range(N_KV_HEADS):
        for j in range(n_top):
            blk = sel_ref[b, kvh * n_top + j]
            page = page_refs[kvh * n_top + j]
            kpos = lax.shift_right_logical(blk, 1) * PAGE_SIZE + lane
            dist = (t_past - kpos).astype(F32)
            valid = (lane_half == (blk & 1)) & (row_kvh == kvh) & (blk < t_past // SLC_BLOCK)
            s = _dot(qb, page[0, 0, 0, 0].astype(BF16)) - slope * dist
            pieces.append((s, valid, page[0, 0, 0, 1].astype(BF16)))
    o_slc = _attend_pieces(pieces, q, snew_ref[0, 0], snew_ref[0, 1])

    pieces = []
    w = win_ref.shape[-1]
    wlane = lax.broadcasted_iota(jnp.int32, (N_HEADS, w), 1)
    wdist = (w - wlane).astype(F32)
    wrow_kvh = lax.shift_right_logical(lax.broadcasted_iota(jnp.int32, (N_HEADS, w), 0), 2)
    for kvh in range(N_KV_HEADS):
        s = _dot(qb, win_ref[0, 0, kvh, 0].astype(BF16)) - slope * wdist
        valid = (wrow_kvh == kvh) & (wdist <= WINDOW)
        pieces.append((s, valid, win_ref[0, 0, kvh, 1].astype(BF16)))
    o_win = _attend_pieces(pieces, q, wnew_ref[0, 0], wnew_ref[0, 1])

    o_ref[0] = gates_ref[0, 0] * oc_ref[0] + gates_ref[0, 1] * o_slc + gates_ref[0, 2] * o_win


def _slc_win_sample(sel, page_table, q8, slc_t, win_t, slc_new, win_new, gates, o_cmp, slopes, layer, *,
                    t_past):
    b, n_pg = page_table.shape
    n_top = sel.shape[1] // N_KV_HEADS
    w = win_t.shape[-1]

    def page_spec(kvh, j):
        def index(bi, sel_ref, pt_ref):
            pg = jnp.minimum(lax.shift_right_logical(sel_ref[bi, kvh * n_top + j], 1), n_pg - 1)
            return (layer, pt_ref[bi, pg], kvh, 0, 0, 0)
        return pl.BlockSpec((1, 1, 1, 2, HEAD_DIM, PAGE_SIZE), index)

    per_seq = lambda *tail: pl.BlockSpec((1,) + tail, lambda bi, s, p: (bi,) + (0,) * len(tail))
    return pl.pallas_call(
        functools.partial(_slc_win_sample_kernel, n_top=n_top, t_past=t_past),
        grid_spec=pltpu.PrefetchScalarGridSpec(
            num_scalar_prefetch=2,
            grid=(b,),
            in_specs=[per_seq(N_HEADS, HEAD_DIM)]
            + [page_spec(kvh, j) for kvh in range(N_KV_HEADS) for j in range(n_top)]
            + [pl.BlockSpec((1, 1, N_KV_HEADS, 2, HEAD_DIM, w), lambda bi, s, p: (layer, bi, 0, 0, 0, 0)),
               per_seq(2, N_HEADS, HEAD_DIM), per_seq(2, N_HEADS, HEAD_DIM), per_seq(3, N_HEADS, 1),
               per_seq(N_HEADS, HEAD_DIM), pl.BlockSpec(slopes.shape, lambda bi, s, p: (0, 0))],
            out_specs=per_seq(N_HEADS, HEAD_DIM),
        ),
        out_shape=jax.ShapeDtypeStruct((b, N_HEADS, HEAD_DIM), F32),
        compiler_params=_cparams(("parallel",)),
        name="slc_win_sample",
    )(sel, page_table, q8, *([slc_t] * (N_KV_HEADS * n_top)), win_t, slc_new, win_new, gates, o_cmp, slopes)


MOE_TOKENS = 256
MOE_SLOTS = 256


def _n_slot_blocks(n_tokens):
    return -(-(2 * n_tokens + N_EXPERTS * (MOE_SLOTS - 1)) // MOE_SLOTS)


def _route_kernel(x_ref, wr_ref, dest_ref, wts_ref, blk_ref, top_ref, count_ref, base_ref, *, n_blocks):
    phase, i = pl.program_id(0), pl.program_id(1)
    tt = x_ref.shape[0]
    rowf = lax.broadcasted_iota(jnp.int32, (N_EXPERTS, tt), 0).astype(F32)
    r8 = lax.broadcasted_iota(jnp.int32, (8, tt), 0)

    @pl.when((phase == 0) & (i == 0))
    def _():
        count_ref[...] = jnp.zeros(count_ref.shape, F32)

    @pl.when(phase == 0)
    def _():
        logits = lax.dot_general(wr_ref[...], x_ref[...], (((1,), (1,)), ((), ())),
                                 precision=lax.Precision.HIGHEST, preferred_element_type=F32)
        m1 = jnp.max(logits, axis=0, keepdims=True)
        e1 = jnp.min(jnp.where(logits == m1, rowf, float(N_EXPERTS)), axis=0, keepdims=True)
        rest = jnp.where(rowf == e1, NEG_INF, logits)
        m2 = jnp.max(rest, axis=0, keepdims=True)
        e2 = jnp.min(jnp.where(rest == m2, rowf, float(N_EXPERTS)), axis=0, keepdims=True)
        top_ref[i] = jnp.where(r8 == 0, e1, jnp.where(r8 == 1, e2, jnp.where(r8 == 2, m1, m2)))
        onehot = jnp.where((rowf == e1) | (rowf == e2), 1.0, 0.0)
        count_ref[...] += jnp.sum(onehot, axis=1, keepdims=True)

    top = top_ref[i]
    e1, e2, m1, m2 = top[0:1], top[1:2], top[2:3], top[3:4]
    hot1, hot2 = rowf == e1, rowf == e2
    onehot = jnp.where(hot1 | hot2, 1.0, 0.0)
    tile_count = jnp.sum(onehot, axis=1, keepdims=True)

    @pl.when((phase == 1) & (i == 0))
    def _():
        padded = jnp.ceil(count_ref[...] / MOE_SLOTS) * MOE_SLOTS
        erow = lax.broadcasted_iota(jnp.int32, padded.shape, 0)
        start = jnp.zeros(padded.shape, F32)
        end = jnp.zeros(padded.shape, F32)
        for e in range(N_EXPERTS):
            pe = padded[e:e + 1, :]
            start = start + jnp.where(erow > e, pe, 0.0)
            end = end + jnp.where(erow >= e, pe, 0.0)
        base_ref[...] = start
        first_row = lax.broadcasted_iota(jnp.int32, (N_EXPERTS, blk_ref.shape[1]), 1).astype(F32) * MOE_SLOTS
        owner = jnp.sum(jnp.where(end[:, 0:1] <= first_row, 1.0, 0.0), axis=0, keepdims=True)
        used = jnp.max(end[:, 0:1], axis=0, keepdims=True) / MOE_SLOTS
        r8 = lax.broadcasted_iota(jnp.int32, blk_ref.shape, 0)
        blk = jnp.where(r8 == 0, jnp.minimum(owner, N_EXPERTS - 1.0), jnp.broadcast_to(used, blk_ref.shape))
        blk_ref[...] = blk.astype(jnp.int32)

    @pl.when(phase == 1)
    def _():
        upper = (lax.broadcasted_iota(jnp.int32, (tt, tt), 0)
                 < lax.broadcasted_iota(jnp.int32, (tt, tt), 1))
        before = _dot(onehot.astype(BF16), jnp.where(upper, 1.0, 0.0).astype(BF16))
        slot = base_ref[:, 0:1] + before
        d1 = jnp.sum(jnp.where(hot1, slot, 0.0), axis=0, keepdims=True)
        d2 = jnp.sum(jnp.where(hot2, slot, 0.0), axis=0, keepdims=True)
        dest_ref[0] = jnp.where(r8 == 0, d1, d2).astype(jnp.int32)
        z = jnp.exp(m2 - m1)
        w1 = 1.0 / (1.0 + z)
        wts_ref[0] = jnp.where(r8 == 0, w1, z * w1)
        base_ref[...] += tile_count


def _route(x, w_router_t):
    n = x.shape[0]
    nt = n // MOE_TOKENS
    n_blocks = _n_slot_blocks(n)
    blk_lanes = -(-n_blocks // LANES) * LANES
    tile = lambda ph, i: (i * ph, 0, 0)
    return pl.pallas_call(
        functools.partial(_route_kernel, n_blocks=n_blocks),
        grid=(2, nt),
        in_specs=[pl.BlockSpec((MOE_TOKENS, D_MODEL), lambda ph, i: (i * (1 - ph) + (nt - 1) * ph, 0)),
                  pl.BlockSpec(w_router_t.shape, lambda ph, i: (0, 0))],
        out_specs=(pl.BlockSpec((1, 8, MOE_TOKENS), tile), pl.BlockSpec((1, 8, MOE_TOKENS), tile),
                   pl.BlockSpec((8, blk_lanes), lambda ph, i: (0, 0))),
        out_shape=(jax.ShapeDtypeStruct((nt, 8, MOE_TOKENS), jnp.int32),
                   jax.ShapeDtypeStruct((nt, 8, MOE_TOKENS), F32),
                   jax.ShapeDtypeStruct((8, blk_lanes), jnp.int32)),
        scratch_shapes=[pltpu.VMEM((nt, 8, MOE_TOKENS), F32), pltpu.VMEM((N_EXPERTS, LANES), F32),
                        pltpu.VMEM((N_EXPERTS, LANES), F32)],
        compiler_params=_cparams(("arbitrary", "arbitrary")),
        name="moe_route",
    )(x, w_router_t)


def _row_copy(src, src_row, dst, dst_row, sem):
    return pltpu.make_async_copy(src.at[pl.ds(src_row, 1), :], dst.at[pl.ds(dst_row, 1), :], sem)


ISSUE_UNROLL = 8


def _drain_rows(src, dst, sem, n):
    def drain(r, carry):
        _row_copy(src, 0, dst, 0, sem).wait()
        return carry
    lax.fori_loop(0, n, drain, 0, unroll=ISSUE_UNROLL)


def _scatter_kernel(d1_ref, d2_ref, x_ref, init_ref, xs_ref, stage_ref, sem):
    del init_ref
    i, n = pl.program_id(0), pl.num_programs(0)
    tt = x_ref.shape[0]
    slot = i % 2

    @pl.when(i >= 2)
    def _():
        _drain_rows(stage_ref.at[slot], xs_ref, sem.at[slot], 2 * tt)

    stage_ref[slot] = x_ref[...]

    def issue(r, carry):
        t = i * tt + r
        _row_copy(stage_ref.at[slot], r, xs_ref, d1_ref[t], sem.at[slot]).start()
        _row_copy(stage_ref.at[slot], r, xs_ref, d2_ref[t], sem.at[slot]).start()
        return carry

    lax.fori_loop(0, tt, issue, 0, unroll=ISSUE_UNROLL)

    @pl.when(i == n - 1)
    def _():
        _drain_rows(stage_ref.at[slot], xs_ref, sem.at[slot], 2 * tt)

        @pl.when(n >= 2)
        def _():
            _drain_rows(stage_ref.at[1 - slot], xs_ref, sem.at[1 - slot], 2 * tt)


def _scatter_rows(x, d1, d2, n_slots):
    n = x.shape[0]
    init = jnp.zeros((n_slots, D_MODEL), F32)
    return pl.pallas_call(
        _scatter_kernel,
        grid_spec=pltpu.PrefetchScalarGridSpec(
            num_scalar_prefetch=2,
            grid=(n // MOE_TOKENS,),
            in_specs=[pl.BlockSpec((MOE_TOKENS, D_MODEL), lambda i, a, b: (i, 0)),
                      pl.BlockSpec(memory_space=pl.ANY)],
            out_specs=pl.BlockSpec(memory_space=pl.ANY),
            scratch_shapes=[pltpu.VMEM((2, MOE_TOKENS, D_MODEL), F32), pltpu.SemaphoreType.DMA((2,))],
        ),
        out_shape=jax.ShapeDtypeStruct((n_slots, D_MODEL), F32),
        input_output_aliases={3: 0},
        compiler_params=_cparams(("arbitrary",)),
        name="moe_scatter",
    )(d1, d2, x, init)


FF_HALF = D_FF // 2


def _expert_ffn_kernel(blk_ref, x_ref, w1_ref, w2_ref, y_ref):
    j = pl.program_id(0)

    @pl.when(j < blk_ref[1, 0])
    def _():
        x = x_ref[...].astype(BF16)
        y = jnp.zeros(y_ref.shape, F32)
        for c in range(2):
            g = _dot(x, w1_ref[0, :, c * FF_HALF:(c + 1) * FF_HALF])
            u = _dot(x, w1_ref[0, :, D_FF + c * FF_HALF:D_FF + (c + 1) * FF_HALF])
            act = (jax.nn.silu(g) * u).astype(BF16)
            y = y + _dot(act, w2_ref[0, c * FF_HALF:(c + 1) * FF_HALF, :])
        y_ref[...] = y

    @pl.when(j >= blk_ref[1, 0])
    def _():
        y_ref[...] = jnp.zeros(y_ref.shape, F32)


def _expert_ffn(xs, blk, w1, w2):
    n_slots = xs.shape[0]
    return pl.pallas_call(
        _expert_ffn_kernel,
        grid_spec=pltpu.PrefetchScalarGridSpec(
            num_scalar_prefetch=1,
            grid=(n_slots // MOE_SLOTS,),
            in_specs=[pl.BlockSpec((MOE_SLOTS, D_MODEL), lambda j, b: (j, 0)),
                      pl.BlockSpec((1, D_MODEL, 2 * D_FF), lambda j, b: (b[0, j], 0, 0)),
                      pl.BlockSpec((1, D_FF, D_MODEL), lambda j, b: (b[0, j], 0, 0))],
            out_specs=pl.BlockSpec((MOE_SLOTS, D_MODEL), lambda j, b: (j, 0)),
        ),
        out_shape=jax.ShapeDtypeStruct((n_slots, D_MODEL), F32),
        compiler_params=_cparams(("arbitrary",)),
        name="moe_expert_ffn",
    )(blk, xs, w1, w2)


def _combine_kernel(d1_ref, d2_ref, ys_ref, x_ref, wts_ref, g_ref, b_ref, o_ref, tail_ref, buf_ref, sem, *,
                    alpha):
    i, n = pl.program_id(0), pl.num_programs(0)
    tt = x_ref.shape[0]
    slot = i % 2

    def gather(tile, slot):
        def issue(r, carry):
            t = tile * tt + r
            _row_copy(ys_ref, d1_ref[t], buf_ref.at[slot, 0], r, sem.at[slot]).start()
            _row_copy(ys_ref, d2_ref[t], buf_ref.at[slot, 1], r, sem.at[slot]).start()
            return carry
        lax.fori_loop(0, tt, issue, 0, unroll=ISSUE_UNROLL)

    @pl.when(i == 0)
    def _():
        gather(0, 0)

    @pl.when(i + 1 < n)
    def _():
        gather(i + 1, 1 - slot)

    _drain_rows(ys_ref, buf_ref.at[slot, 0], sem.at[slot], 2 * tt)
    wcol = wts_ref[0].T
    y = wcol[:, 0:1] * buf_ref[slot, 0] + wcol[:, 1:2] * buf_ref[slot, 1]
    out = _ln(alpha * x_ref[...] + y, g_ref[...], b_ref[...])

    @pl.when(i < n - 1)
    def _():
        o_ref[...] = out

    @pl.when(i == n - 1)
    def _():
        tail_ref[...] = out


def _combine(ys, x, d1, d2, wts, g, b, *, alpha):
    n = x.shape[0]
    nt = n // MOE_TOKENS
    row = pl.BlockSpec((MOE_TOKENS, D_MODEL), lambda i, p, q: (i, 0))
    vec = pl.BlockSpec((1, D_MODEL), lambda i, p, q: (0, 0))
    return pl.pallas_call(
        functools.partial(_combine_kernel, alpha=alpha),
        grid_spec=pltpu.PrefetchScalarGridSpec(
            num_scalar_prefetch=2,
            grid=(nt,),
            in_specs=[pl.BlockSpec(memory_space=pl.ANY), row,
                      pl.BlockSpec((1, 8, MOE_TOKENS), lambda i, p, q: (i, 0, 0)), vec, vec],
            out_specs=(pl.BlockSpec((MOE_TOKENS, D_MODEL), lambda i, p, q: (jnp.minimum(i, nt - 2), 0)),
                       pl.BlockSpec((MOE_TOKENS, D_MODEL), lambda i, p, q: (0, 0))),
            scratch_shapes=[pltpu.VMEM((2, 2, MOE_TOKENS, D_MODEL), F32), pltpu.SemaphoreType.DMA((2,))],
        ),
        out_shape=(jax.ShapeDtypeStruct((n - MOE_TOKENS, D_MODEL), F32),
                   jax.ShapeDtypeStruct((MOE_TOKENS, D_MODEL), F32)),
        compiler_params=_cparams(("arbitrary",)),
        name="moe_combine",
    )(d1, d2, ys, x, wts, g, b)


def _moe_ffn(x, w_router_t, w1, w2, g, b, *, alpha):
    n = x.shape[0]
    dest, wts, blk = _route(x, w_router_t)
    d1, d2 = dest[:, 0, :].reshape(n), dest[:, 1, :].reshape(n)
    xs = _scatter_rows(x, d1, d2, _n_slot_blocks(n) * MOE_SLOTS)
    ys = _expert_ffn(xs, blk, w1, w2)
    return _combine(ys, x, d1, d2, wts, g, b, alpha=alpha)


def _prep_layer(l, w_in, w_phi1, w_phi2, gmlp_ln_g, gmlp_ln_b, gmlp_ws, gmlp_bs, w_out, ln1_g, ln1_b):
    wi = w_in[l]
    c0 = NSA_WIDTH
    c1, c2, c3 = c0 + KV_COLS, c0 + 2 * KV_COLS, c0 + 3 * KV_COLS
    c4 = c3 + N_GATES
    c5 = c4 + GMLP_WIDTH
    eye = jnp.eye(N_KV_HEADS, dtype=F32)
    w1 = w_phi1[l]
    wa = jnp.einsum("cldf,kK,cC->lkcdKCf", w1[:, :CMP_STRIDE], eye, eye)
    wc = jnp.einsum("cldf,kK,cC->lkcdKCf", w1[:, CMP_STRIDE:], eye, eye)
    rows = CMP_STRIDE * KV_COLS
    phi1_rows = jnp.concatenate([wa.reshape(rows, KV_COLS), wc.reshape(rows, KV_COLS)], axis=1)
    phi2_big = jnp.einsum("cfe,kK,cC->kcfKCe", w_phi2[l], eye, eye).reshape(KV_COLS, KV_COLS)
    ws, bs = gmlp_ws[l], gmlp_bs[l]
    g_bias = jnp.repeat(bs.reshape(GMLP_WIDTH // LANES, 2, CHUNK).transpose(0, 2, 1), GMLP_GROUP_DIM, axis=2)
    return {
        "wq": (wi[:, :c0] * SCALE).astype(BF16),
        "wkvt": wi[:, c0:c3].T.astype(BF16),
        "wkvc": wi[:, c0:c1].astype(BF16),
        "wg": jnp.pad(wi[:, c3:c4], ((0, 0), (0, LANES - N_GATES))).astype(BF16),
        "wu": wi[:, c4:c5].astype(BF16),
        "wv": wi[:, c5:].astype(BF16),
        "lng": gmlp_ln_g[l][None, :],
        "lnb": gmlp_ln_b[l][None, :],
        "g_tril": jnp.tril(ws).astype(BF16),
        "g_bias": g_bias,
        "g_diag_w": jnp.repeat(ws[:, 0, 0], GMLP_GROUP_DIM)[None, :],
        "g_diag_b": jnp.repeat(bs[:, 0], GMLP_GROUP_DIM)[None, :],
        "phi1_rows": phi1_rows.astype(BF16),
        "phi2_big": phi2_big.astype(BF16),
        "w_phi1": w1,
        "wo_a": w_out[l][:NSA_WIDTH].astype(BF16),
        "wo_g": w_out[l][NSA_WIDTH:].astype(BF16),
        "ln1_g": ln1_g[l][None, :],
        "ln1_b": ln1_b[l][None, :],
    }


def _kv_leaf(kvt):
    return kvt.reshape(N_KV_HEADS, 2, HEAD_DIM, kvt.shape[-1]).transpose(3, 0, 1, 2)


def _mix_prompt(xp, w, alpha, out_rows=None):
    q, kvt, kvtb, kvc_rows, gates, og, _ = _inproj(xp, w, sample=False, tm=256)
    cmp_kv = _compress_prompt(kvc_rows, w)
    o_cmp, unsel, flags = _cmp_select(q, cmp_kv, tq=128)
    o_attn = _slc_win(q, kvtb, unsel, flags, gates, o_cmp, tq=128)
    x1 = _outproj(o_attn, og, xp, w, alpha=alpha, tm=256, name="outproj_prompt", out_rows=out_rows)
    return x1, kvt


def _cache_view(cache):
    return cache.transpose(0, 1, 3, 4, 5, 2)


def _mix_sample(xs, w, alpha, cmp_t, slc_t, win_t, page_table, layer, append_to=None):
    b = xs.shape[0]
    t_past = page_table.shape[1] * PAGE_SIZE
    q, kvt, _, _, gates, og, vn = _inproj(xs, w, sample=True, tm=b)
    kvt_cmp = _finish_sample(_compress_sample(cmp_t, page_table, w, layer), w)
    q8 = q.astype(F32).reshape(b, N_HEADS, HEAD_DIM)
    kv_new = kvt.T.reshape(b, 3, N_KV_HEADS, 2, HEAD_DIM)
    per_head = lambda kv: jnp.repeat(kv.transpose(0, 2, 1, 3), GROUP, axis=2)
    slopes = jnp.broadcast_to(jnp.exp2(-jnp.arange(1, N_HEADS + 1, dtype=F32))[:, None], (N_HEADS, LANES))
    o_cmp, idx = _cmp_select_sample(q8, kvt_cmp, slopes, t_past=t_past)
    sel = idx[:, ::GROUP, :N_SEL_BLOCKS].reshape(b, N_KV_HEADS * N_SEL_BLOCKS)
    gates8 = gates[:, :N_GATES].reshape(b, 3, N_HEADS, 1)
    o_attn = _slc_win_sample(sel, page_table, q8, slc_t, win_t, per_head(kv_new[:, 1]), per_head(kv_new[:, 2]),
                             gates8, o_cmp, slopes, layer, t_past=t_past)
    o_attn = o_attn.reshape(b, NSA_WIDTH).astype(BF16)
    x1 = _outproj(o_attn, og, xs, w, alpha=alpha, tm=b, name="outproj_sample", append_to=append_to)
    return x1, kv_new, vn


def kernel(x_prompt, x_sample, cache_cmp, cache_slc, cache_win, page_table, w_in, w_phi1, w_phi2, gmlp_ln_g,
           gmlp_ln_b, gmlp_ws, gmlp_bs, w_out, ln1_g, ln1_b, ln2_g, ln2_b, ffn_w1, ffn_w2, moe_router,
           moe_w1, moe_w2):
    depth = w_in.shape[0]
    alpha = (2 * depth) ** 0.25
    t, b = x_prompt.shape[1], x_sample.shape[0]
    xp, xs = x_prompt[0], x_sample[:, 0]
    cmp_t, slc_t, win_t = _cache_view(cache_cmp), _cache_view(cache_slc), _cache_view(cache_win)
    n_win = min(WINDOW, t)
    outs = [[] for _ in range(7)]
    for l in range(depth):
        w = _prep_layer(l, w_in, w_phi1, w_phi2, gmlp_ln_g, gmlp_ln_b, gmlp_ws, gmlp_bs, w_out, ln1_g, ln1_b)
        moe = l % 2 == 1
        x1p, kvt = _mix_prompt(xp, w, alpha, out_rows=t + MOE_TOKENS if moe else None)
        x1s, kv_new, vn = _mix_sample(xs, w, alpha, cmp_t, slc_t, win_t, page_table, l,
                                      append_to=x1p if moe else None)
        outs[0].append(_kv_leaf(kvt[:KV_COLS])[None])
        outs[1].append(_kv_leaf(kvt[KV_COLS:2 * KV_COLS])[None])
        outs[2].append(_kv_leaf(kvt[2 * KV_COLS:, t - n_win:])[None])
        for k in range(3):
            outs[3 + k].append(kv_new[:, k][:, None])
        outs[6].append(vn[:, None, :])
        g2, b2 = ln2_g[l][None, :], ln2_b[l][None, :]
        i = l // 2
        if not moe:
            w1, w2 = ffn_w1[i].astype(BF16), ffn_w2[i].astype(BF16)
            xp = _ffn_dense(x1p, w1, w2, g2, b2, alpha=alpha, tm=256, name="ffn_prompt")
            xs = _ffn_dense(x1s, w1, w2, g2, b2, alpha=alpha, tm=b, name="ffn_sample")
        else:
            xp, tail = _moe_ffn(x1s, moe_router[i].T, moe_w1[i].astype(BF16), moe_w2[i].astype(BF16), g2, b2,
                                alpha=alpha)
            xs = tail[:b]
    return (xp[None], xs[:, None]) + tuple(jnp.stack(o) for o in outs)
```

```python
import functools

import jax
import jax.numpy as jnp
import ml_dtypes
import numpy as np
from jax import lax
from jax.experimental import pallas as pl
from jax.experimental.pallas import tpu as pltpu

F32 = jnp.float32
BF16 = jnp.bfloat16

D_MODEL = 1024
N_HEADS = 8
N_KV_HEADS = 2
GROUP = N_HEADS // N_KV_HEADS
HEAD_DIM = 64
NSA_WIDTH = N_HEADS * HEAD_DIM
SCALE = HEAD_DIM ** -0.5
CMP_STRIDE = 16
CMP_LEN = 32
SLC_BLOCK = 64
N_SEL_BLOCKS = 16
WINDOW = 512
GMLP_GROUPS = 8
GMLP_GROUP_DIM = 64
GMLP_WIDTH = GMLP_GROUPS * GMLP_GROUP_DIM
CHUNK = 128
KV_COLS = N_KV_HEADS * 2 * HEAD_DIM
N_GATES = 3 * N_HEADS
D_FF = 2816
N_EXPERTS = 8
PAGE_SIZE = 128
LN_EPS = 1e-5

LANES = 128
VMEM_LIMIT = 56 * 1024 * 1024


def _cparams(sem, vmem=VMEM_LIMIT):
    return pltpu.CompilerParams(dimension_semantics=sem, vmem_limit_bytes=vmem)


def _ln(x, g, b):
    xc = x - jnp.mean(x, axis=-1, keepdims=True)
    var = jnp.mean(xc * xc, axis=-1, keepdims=True)
    return xc * lax.rsqrt(var + LN_EPS) * g + b


def _dot(a, b):
    return jnp.dot(a, b, preferred_element_type=F32)


def _dot_nt(a, b):
    return lax.dot_general(a, b, (((1,), (1,)), ((), ())), preferred_element_type=F32)


def _full(shape):
    n = len(shape)
    return pl.BlockSpec(shape, lambda *_: (0,) * n)


def _inproj_kernel(x_ref, wq_ref, wkvt_ref, wkvc_ref, wg_ref, wu_ref, wv_ref, lng_ref, lnb_ref,
                   gw_ref, gb_ref, q_ref, kvt_ref, kvtb_ref, kvc_ref, gates_ref, og_ref, vn_ref, *,
                   sample):
    x = x_ref[...].astype(BF16)
    q_ref[...] = _dot(x, wq_ref[...]).astype(q_ref.dtype)
    kvt = _dot_nt(wkvt_ref[...], x)
    kvt_ref[...] = kvt
    kvtb_ref[...] = kvt[KV_COLS:].astype(kvtb_ref.dtype)
    kvc_ref[...] = _dot(x, wkvc_ref[...]).astype(kvc_ref.dtype)
    gates_ref[...] = jax.nn.sigmoid(_dot(x, wg_ref[...]))
    u = jax.nn.gelu(_dot(x, wu_ref[...]))
    v = jax.nn.gelu(_dot(x, wv_ref[...]))
    vn = _ln(v, lng_ref[...], lnb_ref[...])
    vn_ref[...] = vn
    if sample:
        og_ref[...] = (u * (vn * gw_ref[...] + gb_ref[...])).astype(og_ref.dtype)
    else:
        tm = x.shape[0]
        lane = lax.broadcasted_iota(jnp.int32, (CHUNK, LANES), 1)
        first = lane < GMLP_GROUP_DIM
        vb = vn.astype(BF16)
        for c in range(tm // CHUNK):
            rows = slice(c * CHUNK, (c + 1) * CHUNK)
            for p in range(GMLP_WIDTH // LANES):
                cols = slice(p * LANES, (p + 1) * LANES)
                vp = vb[rows, cols]
                mix = jnp.where(first, _dot(gw_ref[2 * p], vp), _dot(gw_ref[2 * p + 1], vp))
                og_ref[rows, cols] = (u[rows, cols] * (mix + gb_ref[p])).astype(og_ref.dtype)


def _inproj(x, w, *, sample, tm):
    rows = x.shape[0]
    grid = (rows // tm,)
    row = lambda n: pl.BlockSpec((tm, n), lambda i: (i, 0))
    if sample:
        gw_spec, gb_spec = _full((1, GMLP_WIDTH)), _full((1, GMLP_WIDTH))
        gw, gb = w["g_diag_w"], w["g_diag_b"]
    else:
        gw_spec, gb_spec = _full((GMLP_GROUPS, CHUNK, CHUNK)), _full((GMLP_WIDTH // LANES, CHUNK, LANES))
        gw, gb = w["g_tril"], w["g_bias"]
    out_shape = (
        jax.ShapeDtypeStruct((rows, NSA_WIDTH), BF16),
        jax.ShapeDtypeStruct((3 * KV_COLS, rows), F32),
        jax.ShapeDtypeStruct((2 * KV_COLS, rows), BF16),
        jax.ShapeDtypeStruct((rows, KV_COLS), BF16),
        jax.ShapeDtypeStruct((rows, LANES), F32),
        jax.ShapeDtypeStruct((rows, GMLP_WIDTH), BF16),
        jax.ShapeDtypeStruct((rows, GMLP_WIDTH), F32),
    )
    return pl.pallas_call(
        functools.partial(_inproj_kernel, sample=sample),
        grid=grid,
        in_specs=[row(D_MODEL), _full((D_MODEL, NSA_WIDTH)), _full((3 * KV_COLS, D_MODEL)),
                  _full((D_MODEL, KV_COLS)), _full((D_MODEL, LANES)), _full((D_MODEL, GMLP_WIDTH)),
                  _full((D_MODEL, GMLP_WIDTH)), _full((1, GMLP_WIDTH)), _full((1, GMLP_WIDTH)),
                  gw_spec, gb_spec],
        out_specs=(row(NSA_WIDTH), pl.BlockSpec((3 * KV_COLS, tm), lambda i: (0, i)),
                   pl.BlockSpec((2 * KV_COLS, tm), lambda i: (0, i)), row(KV_COLS),
                   row(LANES), row(GMLP_WIDTH), row(GMLP_WIDTH)),
        out_shape=out_shape,
        compiler_params=_cparams(("parallel",)),
        name="inproj_sample" if sample else "inproj_prompt",
    )(x, w["wq"], w["wkvt"], w["wkvc"], w["wg"], w["wu"], w["wv"], w["lng"], w["lnb"], gw, gb)


def _outproj_kernel(oa_ref, og_ref, x_ref, wa_ref, wg_ref, g_ref, b_ref, *rest, alpha, n_tiles):
    o_ref = rest[-1]
    rows = x_ref.shape[0]

    @pl.when(pl.program_id(0) < n_tiles)
    def _():
        mixed = _dot(oa_ref[...], wa_ref[...]) + _dot(og_ref[...], wg_ref[...])
        o_ref[:rows, :] = _ln(alpha * x_ref[...] + mixed, g_ref[...], b_ref[...])
        if o_ref.shape[0] > rows:
            o_ref[rows:, :] = jnp.zeros((o_ref.shape[0] - rows, D_MODEL), F32)

    @pl.when(pl.program_id(0) >= n_tiles)
    def _():
        o_ref[...] = jnp.zeros(o_ref.shape, F32)


def _outproj(oa, og, x, w, *, alpha, tm, name, out_rows=None, append_to=None):
    rows = x.shape[0]
    n_tiles = rows // tm
    row = lambda n: pl.BlockSpec((tm, n), lambda i: (jnp.minimum(i, n_tiles - 1), 0))
    in_specs = [row(NSA_WIDTH), row(GMLP_WIDTH), row(D_MODEL), _full((NSA_WIDTH, D_MODEL)),
                _full((GMLP_WIDTH, D_MODEL)), _full((1, D_MODEL)), _full((1, D_MODEL))]
    args = [oa, og, x, w["wo_a"], w["wo_g"], w["ln1_g"], w["ln1_b"]]
    grid = (n_tiles,)
    if append_to is None:
        out_spec = pl.BlockSpec((tm, D_MODEL), lambda i: (i, 0))
        out_shape = jax.ShapeDtypeStruct((out_rows or rows, D_MODEL), F32)
        grid = ((out_rows or rows) // tm,)
        aliases = {}
    else:
        start_tile = (append_to.shape[0] - MOE_TOKENS) // MOE_TOKENS
        in_specs.append(pl.BlockSpec(memory_space=pl.ANY))
        args.append(append_to)
        out_spec = pl.BlockSpec((MOE_TOKENS, D_MODEL), lambda i: (start_tile, 0))
        out_shape = jax.ShapeDtypeStruct(append_to.shape, F32)
        aliases = {len(args) - 1: 0}
    return pl.pallas_call(
        functools.partial(_outproj_kernel, alpha=alpha, n_tiles=n_tiles),
        grid=grid,
        in_specs=in_specs,
        out_specs=out_spec,
        out_shape=out_shape,
        input_output_aliases=aliases,
        compiler_params=_cparams(("parallel",)),
        name=name,
    )(*args)


def _ffn_kernel(x_ref, w1_ref, w2_ref, g_ref, b_ref, o_ref, *, alpha):
    x = x_ref[...]
    h = _dot(x.astype(BF16), w1_ref[...])
    act = (jax.nn.silu(h[:, :D_FF]) * h[:, D_FF:]).astype(BF16)
    y = _dot(act, w2_ref[...])
    o_ref[...] = _ln(alpha * x + y, g_ref[...], b_ref[...])


def _ffn_dense(x, w1, w2, g, b, *, alpha, tm, name):
    rows = x.shape[0]
    row = lambda n: pl.BlockSpec((tm, n), lambda i: (i, 0))
    return pl.pallas_call(
        functools.partial(_ffn_kernel, alpha=alpha),
        grid=(rows // tm,),
        in_specs=[row(D_MODEL), _full((D_MODEL, 2 * D_FF)), _full((D_FF, D_MODEL)),
                  _full((1, D_MODEL)), _full((1, D_MODEL))],
        out_specs=row(D_MODEL),
        out_shape=jax.ShapeDtypeStruct((rows, D_MODEL), F32),
        compiler_params=_cparams(("parallel",)),
        name=name,
    )(x, w1, w2, g, b)


def _finish_compress(ac, w2):
    n = ac.shape[0]
    a, c = ac[:, :KV_COLS], ac[:, KV_COLS:]
    row = lax.broadcasted_iota(jnp.int32, (n, KV_COLS), 0)
    c_next = jnp.where(row == n - 1, 0.0, pltpu.roll(c, n - 1, 0))
    h = jax.nn.gelu(a + c_next)
    return _dot(h.astype(BF16), w2)


def _compress_prompt_kernel(xs_ref, w1_ref, w2_ref, o_ref):
    o_ref[...] = _finish_compress(_dot(xs_ref[...], w1_ref[...]), w2_ref[...])


def _compress_prompt(kvc_rows, w):
    t = kvc_rows.shape[0]
    n_sub = t // CMP_STRIDE
    xs = kvc_rows.reshape(n_sub, CMP_STRIDE * KV_COLS)
    return pl.pallas_call(
        _compress_prompt_kernel,
        grid=(1,),
        in_specs=[_full(xs.shape), _full(w["phi1_rows"].shape), _full(w["phi2_big"].shape)],
        out_specs=_full((n_sub, KV_COLS)),
        out_shape=jax.ShapeDtypeStruct((n_sub, KV_COLS), F32),
        compiler_params=_cparams(("arbitrary",)),
        name="compress_prompt",
    )(xs, w["phi1_rows"], w["phi2_big"])


NEG_INF = float("-inf")


def _slope(head):
    return 2.0 ** -(head + 1)


def _cmp_probs(s, dist, valid, slope):
    s = jnp.where(valid, s - slope * dist, NEG_INF)
    m = jnp.max(s, axis=-1, keepdims=True)
    m = jnp.where(jnp.isfinite(m), m, 0.0)
    e = jnp.exp(s - m)
    return e / jnp.maximum(jnp.sum(e, axis=-1, keepdims=True), 1e-30)


def _select_blocks(pooled, cur, n_sel):
    blk = lax.broadcasted_iota(jnp.int32, pooled.shape, 1)
    blkf = blk.astype(F32)
    forced = (blk == 0) | (blk == cur) | (blk == cur - 1)
    work = jnp.where(blk <= cur, jnp.where(forced, jnp.inf, pooled), NEG_INF)
    sel = jnp.zeros(pooled.shape, F32)
    picks = []
    for _ in range(min(N_SEL_BLOCKS, n_sel)):
        m = jnp.max(work, axis=-1, keepdims=True)
        idx = jnp.min(jnp.where(work == m, blkf, 1e9), axis=-1, keepdims=True)
        pick = blkf == idx
        sel = jnp.maximum(sel, jnp.where(pick, jnp.where(m > NEG_INF, 1.0, 0.0), 0.0))
        work = jnp.where(pick, NEG_INF, work)
        picks.append(idx)
    return sel, picks


def _select_mask(pooled, cur, tri):
    n_top = min(N_SEL_BLOCKS, pooled.shape[1])
    blk = lax.broadcasted_iota(jnp.int32, pooled.shape, 1)
    forced = (blk == 0) | (blk == cur) | (blk == cur - 1)
    work = jnp.where(blk <= cur, jnp.where(forced, jnp.inf, pooled), NEG_INF)
    never = float(n_top)
    rnd = jnp.full(pooled.shape, never + 1.0, F32)
    counts = []
    for r in range(n_top):
        m = jnp.max(work, axis=-1, keepdims=True)
        pick = work == m
        live = m > NEG_INF
        rnd = jnp.where(pick, jnp.where(live, float(r), rnd), rnd)
        counts.append(jnp.sum(jnp.where(pick, jnp.where(live, 1.0, 0.0), 0.0), axis=-1, keepdims=True))
        work = jnp.where(pick, NEG_INF, work)
    taken = jnp.zeros((pooled.shape[0], 1), F32)
    last = jnp.full((pooled.shape[0], 1), never, F32)
    taken_before_last = jnp.zeros((pooled.shape[0], 1), F32)
    for r in range(n_top):
        crosses = (taken < n_top) & (taken + counts[r] >= n_top)
        last = jnp.where(crosses, float(r), last)
        taken_before_last = jnp.where(crosses, taken, taken_before_last)
        taken = taken + counts[r]
    in_last = jnp.where(rnd == last, 1.0, 0.0)
    lower_ties = _dot(in_last.astype(BF16), tri)
    return jnp.where(rnd < last, 1.0, in_last * jnp.where(taken_before_last + lower_ties < n_top, 1.0, 0.0))


ROW_GROUP = 32


def _split3(v):
    to_bf16 = lambda a: a.astype(ml_dtypes.bfloat16).astype(np.float32)
    hi = to_bf16(v)
    mid = to_bf16(v - hi)
    lo = to_bf16(v - hi - mid)
    return hi, mid, lo


def _masked_probs(s, valid):
    s = jnp.where(valid, s, NEG_INF)
    m = jnp.max(s, axis=-1, keepdims=True)
    m = jnp.where(jnp.isfinite(m), m, 0.0)
    e = jnp.exp(s - m)
    return e / jnp.maximum(jnp.sum(e, axis=-1, keepdims=True), 1e-30)


def _cmp_select_kernel(q_ref, kbd_ref, vl_ref, vr_ref, cpos_ref, tri_ref, oc_ref, unsel_ref, flag_ref,
                       s_ref, p_ref, sel_ref, *, n_sel, tq):
    n_cmp = 4 * n_sel
    q0 = pl.program_id(0) * tq
    pairs = [(kvh, p) for kvh in range(N_KV_HEADS) for p in range(GROUP // 2)]
    frow = lax.broadcasted_iota(jnp.int32, (tq, LANES), 0).astype(F32)
    flane = lax.broadcasted_iota(jnp.int32, (tq, LANES), 1)
    feats = jnp.where(flane == 0, frow, jnp.where(flane < 4, 1.0, 0.0)).astype(BF16)
    for kvh, p in pairs:
        cols = slice((kvh * 2 + p) * LANES, (kvh * 2 + p + 1) * LANES)
        s2 = _dot(jnp.concatenate([q_ref[:, cols], feats], axis=1), kbd_ref[kvh, p])
        for j in range(2):
            s_ref[kvh * GROUP + 2 * p + j] = s2[:, j * n_cmp:(j + 1) * n_cmp]
    for g in range(tq // ROW_GROUP):
        rows = slice(g * ROW_GROUP, (g + 1) * ROW_GROUP)
        qpos = q0 + g * ROW_GROUP + lax.broadcasted_iota(jnp.int32, (ROW_GROUP, n_cmp), 0)
        valid = cpos_ref[...] <= qpos.astype(F32)
        cur = lax.shift_right_logical(
            q0 + g * ROW_GROUP + lax.broadcasted_iota(jnp.int32, (ROW_GROUP, n_sel), 0), 6)
        for kvh in range(N_KV_HEADS):
            imp = jnp.zeros((ROW_GROUP, n_cmp), F32)
            for h in range(kvh * GROUP, (kvh + 1) * GROUP):
                pr = _masked_probs(s_ref[h, rows, :], valid)
                p_ref[h, rows, :] = pr.astype(BF16)
                imp = imp + pr
            pooled = imp[:, :n_sel]
            for r in range(1, 4):
                pooled = pooled + imp[:, r * n_sel:(r + 1) * n_sel]
            sel_ref[kvh, rows, :] = _select_mask(pooled, cur, tri_ref[...])
    for kvh, p in pairs:
        cols = slice((kvh * 2 + p) * LANES, (kvh * 2 + p + 1) * LANES)
        h = kvh * GROUP + 2 * p
        oc_ref[:, cols] = _dot(p_ref[h], vl_ref[kvh]) + _dot(p_ref[h + 1], vr_ref[kvh])
    for kvh in range(N_KV_HEADS):
        sel = sel_ref[kvh]
        unsel_ref[kvh] = (1.0 - sel).T
        flag_ref[0, kvh:kvh + 1, :] = jnp.max(sel, axis=0, keepdims=True).astype(jnp.int32)


def _cmp_select(q, cmp_kv, *, tq):
    t = q.shape[0]
    n_cmp = cmp_kv.shape[0]
    n_sel = n_cmp // 4
    perm = cmp_kv.reshape(n_sel, 4, N_KV_HEADS, 2, HEAD_DIM).transpose(2, 3, 1, 0, 4)
    perm = perm.reshape(N_KV_HEADS, 2, n_cmp, HEAD_DIM)
    kc, vc = perm[:, 0], perm[:, 1]
    kct = jnp.swapaxes(kc, 1, 2).astype(BF16)
    z = jnp.zeros_like(kct)
    kbd = jnp.concatenate([jnp.concatenate([kct, z], axis=2), jnp.concatenate([z, kct], axis=2)], axis=1)
    zv = jnp.zeros_like(vc)
    vl = jnp.concatenate([vc, zv], axis=2).astype(BF16)
    vr = jnp.concatenate([zv, vc], axis=2).astype(BF16)
    col = np.arange(n_cmp)
    cpos_np = ((4 * (col % n_sel) + col // n_sel) * CMP_STRIDE + (CMP_LEN - 1)).astype(np.float32)
    cpos = jnp.asarray(cpos_np)[None, :]
    bias = np.zeros((N_KV_HEADS, GROUP // 2, LANES, 2 * n_cmp), np.float32)
    for kvh in range(N_KV_HEADS):
        for p in range(GROUP // 2):
            for j in range(2):
                slope = np.float32(_slope(kvh * GROUP + 2 * p + j))
                cs = slice(j * n_cmp, (j + 1) * n_cmp)
                bias[kvh, p, 0, cs] = -slope
                bias[kvh, p, 1, cs], bias[kvh, p, 2, cs], bias[kvh, p, 3, cs] = _split3(slope * cpos_np)
    kbd = jnp.concatenate([jnp.broadcast_to(kbd[:, None], (N_KV_HEADS, GROUP // 2) + kbd.shape[1:]),
                           jnp.asarray(bias, BF16)], axis=2)
    tri = jnp.asarray(np.triu(np.ones((n_sel, n_sel), np.float32), 1), BF16)
    row = lambda n: pl.BlockSpec((tq, n), lambda i: (i, 0))
    return pl.pallas_call(
        functools.partial(_cmp_select_kernel, n_sel=n_sel, tq=tq),
        grid=(t // tq,),
        in_specs=[row(NSA_WIDTH), _full(kbd.shape), _full(vl.shape), _full(vr.shape), _full(cpos.shape),
                  _full(tri.shape)],
        out_specs=(row(NSA_WIDTH), pl.BlockSpec((N_KV_HEADS, n_sel, tq), lambda i: (0, 0, i)),
                   pl.BlockSpec((1, N_KV_HEADS, n_sel), lambda i: (i, 0, 0))),
        out_shape=(jax.ShapeDtypeStruct((t, NSA_WIDTH), F32),
                   jax.ShapeDtypeStruct((N_KV_HEADS, n_sel, t), F32),
                   jax.ShapeDtypeStruct((t // tq, N_KV_HEADS, n_sel), jnp.int32)),
        scratch_shapes=[pltpu.VMEM((N_HEADS, tq, n_cmp), F32), pltpu.VMEM((N_HEADS, tq, n_cmp), BF16),
                        pltpu.VMEM((N_KV_HEADS, tq, n_sel), F32)],
        compiler_params=_cparams(("parallel",)),
        name="cmp_select_prompt",
    )(q, kbd, vl, vr, cpos, tri)


MASKED = -1e30


MASK_BIG = 2.0 ** 100
M_FLOOR = -(2.0 ** 99)
KIND_PLAIN, KIND_CAUSAL, KIND_EDGE, KIND_DUMMY = 0, 1, 2, 3
N_FEAT = 16
SLOTS = 3
MASKED_UNITS = 2
UNITS_PER_SLOT = 2


def _unit_code(kind, chunk):
    return kind * 256 + chunk


def _bias_rows():
    lane = np.arange(2 * LANES)
    key = lane % LANES
    out = np.zeros((N_KV_HEADS, 4, GROUP // 2, N_FEAT, 2 * LANES), np.float32)
    for kvh in range(N_KV_HEADS):
        for pair in range(GROUP // 2):
            slope = 2.0 ** -(kvh * GROUP + 2 * pair + lane // LANES + 1.0)
            for cm in range(4):
                for f in range(8):
                    out[kvh, cm, pair, f] = np.where(f == 2 * cm + key // SLC_BLOCK, -MASK_BIG, 0.0)
                out[kvh, cm, pair, 8] = -slope
                out[kvh, cm, pair, 9] = slope * key
                out[kvh, cm, pair, 10] = slope * LANES
    return jnp.asarray(out, BF16)


def _gate_expansion():
    out = np.zeros((LANES, 3 * NSA_WIDTH), np.float32)
    col = np.arange(3 * NSA_WIDTH)
    out[col // HEAD_DIM, col] = 1.0
    return jnp.asarray(out, BF16)


def _unit_scores(code, i, kvh, branch, q_ref, kv_ref, unselt_ref, rhsx_ref, mb_ref, const8, rp_ref, vw_ref,
                 xt_ref):
    kind = lax.shift_right_logical(code, 8)
    c = code & 255
    lanes = pl.ds(pl.multiple_of(c * LANES, LANES), LANES)
    row0 = branch * KV_COLS + kvh * 2 * HEAD_DIM
    kt = kv_ref[row0:row0 + HEAD_DIM, lanes]
    vt = kv_ref[row0 + HEAD_DIM:row0 + 2 * HEAD_DIM, lanes]
    vw_ref[:HEAD_DIM, :LANES] = vt
    vw_ref[LANES + HEAD_DIM:, LANES:] = vt
    dummy = jnp.where(kind == KIND_DUMMY, 1.0, 0.0)
    if branch == 0:
        blocks8 = pl.ds(pl.multiple_of(lax.shift_right_logical(c, 2) * 8, 8), 8)
        xt_ref[0:8, :] = unselt_ref[kvh, blocks8, :] * (1.0 - dummy) + dummy
    else:
        xt_ref[0:8, :] = jnp.zeros((8, xt_ref.shape[1]), F32) + dummy
    row8 = lax.broadcasted_iota(jnp.int32, const8.shape, 0)
    xt_ref[8:16, :] = const8 + jnp.where(row8 == 2, (c - i).astype(F32), 0.0)
    x = xt_ref[...].T.astype(BF16)
    scores = []
    for pair in range(GROUP // 2):
        rp_ref[pair, :HEAD_DIM, :LANES] = kt
        rp_ref[pair, HEAD_DIM:LANES, LANES:] = kt
        rp_ref[pair, LANES:LANES + N_FEAT, :] = rhsx_ref[kvh, c & 3, pair]
        cols = slice((kvh * 2 + pair) * LANES, (kvh * 2 + pair + 1) * LANES)
        scores.append(_dot(jnp.concatenate([q_ref[:, cols], x], axis=1), rp_ref[pair]))
    return scores, mb_ref[jnp.where(kind == KIND_DUMMY, 0, kind)]


def _unit_probs(scores, mb, m_ref):
    out = []
    for pair, s2 in enumerate(scores):
        ps, alphas = [], []
        for j in range(2):
            h = 2 * pair + j
            s = s2[:, j * LANES:(j + 1) * LANES]
            if mb is not None:
                s = s + mb
            m_old = m_ref[h]
            m_new = jnp.maximum(m_old, jnp.max(s, axis=-1, keepdims=True))
            ps.append(jnp.exp(s - m_new).astype(BF16))
            alphas.append(jnp.exp(m_old - m_new))
            m_ref[h] = m_new
        out.append((jnp.concatenate(ps, axis=1), jnp.concatenate(alphas, axis=1)))
    return out


def _unit_accumulate(probs, vw_ref, acc_ref):
    for pair, (p2, alpha2) in enumerate(probs):
        acc_ref[pair] = alpha2 * acc_ref[pair] + _dot_nt(p2, vw_ref[...])


def _build_unit_lists(kvh, i, flag_ref, list_ref):
    a, b, w = kvh * SLOTS, kvh * SLOTS + 1, kvh * SLOTS + 2
    n_win = WINDOW // LANES
    list_ref[a, 0] = _unit_code(KIND_CAUSAL, i)
    list_ref[w, 0] = _unit_code(KIND_CAUSAL, i)
    n_w = jnp.int32(1)
    for d in [n_win] + list(range(1, n_win)):
        list_ref[w, n_w] = _unit_code(KIND_EDGE if d == n_win else KIND_PLAIN, jnp.maximum(i - d, 0))
        n_w = n_w + (i >= d).astype(jnp.int32)

    def scan(c, counts):
        n_a, n_b = counts
        code = _unit_code(KIND_PLAIN, c)
        list_ref[a, n_a] = code
        list_ref[b, n_b] = code
        picked = ((flag_ref[0, kvh, 2 * c] + flag_ref[0, kvh, 2 * c + 1]) > 0).astype(jnp.int32)
        to_a = (n_a <= n_b).astype(jnp.int32)
        return n_a + picked * to_a, n_b + picked * (1 - to_a)

    n_a, n_b = lax.fori_loop(0, i, scan, (jnp.int32(1), jnp.int32(0)))
    return n_a, n_b, n_w


def _attend_result(acc, lane):
    l_even = jnp.maximum(acc[:, HEAD_DIM:HEAD_DIM + 1], 1e-30)
    l_odd = jnp.maximum(acc[:, LANES:LANES + 1], 1e-30)
    return jnp.where(lane < HEAD_DIM, acc[:, :LANES] / l_even, acc[:, LANES:] / l_odd)


def _slc_win_kernel(flag_ref, q_ref, kv_ref, unselt_ref, rhsx_ref, egate_ref, gates_ref, oc_ref, o_ref,
                    list_ref, rp_ref, vw_ref, xt_ref, mb_ref, m_ref, acc_ref, *, tq):
    i = pl.program_id(0)
    n_slots = N_KV_HEADS * SLOTS
    row = lax.broadcasted_iota(jnp.int32, (tq, LANES), 0)
    lane = lax.broadcasted_iota(jnp.int32, (tq, LANES), 1)

    @pl.when(i == 0)
    def _():
        mb_ref[KIND_PLAIN] = jnp.zeros((tq, LANES), F32)
        mb_ref[KIND_CAUSAL] = jnp.where(lane <= row, 0.0, -MASK_BIG)
        mb_ref[KIND_EDGE] = jnp.where(lane >= row, 0.0, -MASK_BIG)
        rp_ref[...] = jnp.zeros(rp_ref.shape, rp_ref.dtype)
        vrow = lax.broadcasted_iota(jnp.int32, (2 * LANES, 2 * LANES), 0)
        vcol = lax.broadcasted_iota(jnp.int32, (2 * LANES, 2 * LANES), 1)
        ones_rows = ((vrow >= HEAD_DIM) & (vrow < LANES + HEAD_DIM)) & ((vrow < LANES) == (vcol < LANES))
        for s in range(vw_ref.shape[0]):
            vw_ref[s] = jnp.where(ones_rows, 1.0, 0.0).astype(vw_ref.dtype)
        xt_ref[...] = jnp.zeros(xt_ref.shape, F32)

    m_ref[...] = jnp.full(m_ref.shape, M_FLOOR, F32)
    acc_ref[...] = jnp.zeros(acc_ref.shape, F32)
    row8 = lax.broadcasted_iota(jnp.int32, (8, tq), 0)
    q_index = lax.broadcasted_iota(jnp.int32, (8, tq), 1).astype(F32)
    const8 = jnp.where(row8 == 0, q_index, jnp.where(row8 == 1, 1.0, 0.0))

    counts = []
    for kvh in range(N_KV_HEADS):
        counts.extend(_build_unit_lists(kvh, i, flag_ref, list_ref))
    n_iter = (functools.reduce(jnp.maximum, counts) + UNITS_PER_SLOT - 1) // UNITS_PER_SLOT
    for s in range(n_slots):
        def pad(t, carry, s=s):
            list_ref[s, t] = _unit_code(KIND_DUMMY, 0)
            return carry
        lax.fori_loop(counts[s], n_iter * UNITS_PER_SLOT, pad, 0)

    def body(t, carry, masked):
        for kvh in range(N_KV_HEADS):
            units = [(s, u) for u in range(UNITS_PER_SLOT) for s in range(kvh * SLOTS, (kvh + 1) * SLOTS)]
            buf = lambda s, u: s * UNITS_PER_SLOT + u
            scored = [_unit_scores(list_ref[s, t * UNITS_PER_SLOT + u], i, kvh, int(s % SLOTS == SLOTS - 1), q_ref,
                                   kv_ref, unselt_ref, rhsx_ref, mb_ref, const8, rp_ref.at[buf(s, u)],
                                   vw_ref.at[buf(s, u)], xt_ref.at[buf(s, u)])
                      for s, u in units]
            probs = [_unit_probs(scores, mb if masked else None, m_ref.at[s])
                     for (s, u), (scores, mb) in zip(units, scored)]
            for (s, u), pr in zip(units, probs):
                _unit_accumulate(pr, vw_ref.at[buf(s, u)], acc_ref.at[s])
        return carry

    n_masked = jnp.minimum(n_iter, -(-MASKED_UNITS // UNITS_PER_SLOT))
    lax.fori_loop(0, n_masked, functools.partial(body, masked=True), 0)
    lax.fori_loop(n_masked, n_iter, functools.partial(body, masked=False), 0)

    gates = gates_ref[...]
    g_hi = gates.astype(BF16)
    g_lo = (gates - g_hi.astype(F32)).astype(BF16)
    g = _dot(g_hi, egate_ref[...]) + _dot(g_lo, egate_ref[...])
    for kvh in range(N_KV_HEADS):
        a, b, w = kvh * SLOTS, kvh * SLOTS + 1, kvh * SLOTS + 2
        for pair in range(GROUP // 2):
            m_a = jnp.concatenate([m_ref[a, 2 * pair], m_ref[a, 2 * pair + 1]], axis=1)
            m_b = jnp.concatenate([m_ref[b, 2 * pair], m_ref[b, 2 * pair + 1]], axis=1)
            m = jnp.maximum(m_a, m_b)
            slc = jnp.exp(m_a - m) * acc_ref[a, pair] + jnp.exp(m_b - m) * acc_ref[b, pair]
            c0 = (kvh * 2 + pair) * LANES
            o_ref[:, c0:c0 + LANES] = (
                g[:, c0:c0 + LANES] * oc_ref[:, c0:c0 + LANES]
                + g[:, NSA_WIDTH + c0:NSA_WIDTH + c0 + LANES] * _attend_result(slc, lane)
                + g[:, 2 * NSA_WIDTH + c0:2 * NSA_WIDTH + c0 + LANES] * _attend_result(acc_ref[w, pair], lane)
            ).astype(o_ref.dtype)


def _slc_win(q, kvtb, unselt, flags, gates, o_cmp, *, tq):
    t = q.shape[0]
    n_sel = unselt.shape[1]
    rhsx, egate = _bias_rows(), _gate_expansion()
    n_slots = N_KV_HEADS * SLOTS
    n_bufs = n_slots * UNITS_PER_SLOT
    max_units = 8 + t // LANES
    row = lambda n: pl.BlockSpec((tq, n), lambda i: (i, 0))
    return pl.pallas_call(
        functools.partial(_slc_win_kernel, tq=tq),
        grid=(t // tq,),
        in_specs=[pl.BlockSpec((1, N_KV_HEADS, n_sel), lambda i: (i, 0, 0), memory_space=pltpu.SMEM),
                  row(NSA_WIDTH), _full(kvtb.shape),
                  pl.BlockSpec((N_KV_HEADS, n_sel, tq), lambda i: (0, 0, i)), _full(rhsx.shape),
                  _full(egate.shape), row(LANES), row(NSA_WIDTH)],
        out_specs=row(NSA_WIDTH),
        out_shape=jax.ShapeDtypeStruct((t, NSA_WIDTH), BF16),
        scratch_shapes=[pltpu.SMEM((n_slots, max_units), jnp.int32),
                        pltpu.VMEM((n_bufs, GROUP // 2, 2 * LANES, 2 * LANES), BF16),
                        pltpu.VMEM((n_bufs, 2 * LANES, 2 * LANES), BF16),
                        pltpu.VMEM((n_bufs, LANES, tq), F32),
                        pltpu.VMEM((3, tq, LANES), F32),
                        pltpu.VMEM((n_slots, GROUP, tq, LANES), F32),
                        pltpu.VMEM((n_slots, GROUP // 2, tq, 2 * LANES), F32)],
        compiler_params=_cparams(("arbitrary",)),
        name="slc_win_prompt",
    )(flags, q, kvtb, unselt, rhsx, egate, gates, o_cmp)


PAGES_PER_STEP = 32


def _compress_sample_kernel(pt_ref, *refs, n_pages):
    page_refs, (perm_ref, w1_ref, ac_ref, xs_ref) = refs[:n_pages], refs[n_pages:]
    per_page = PAGE_SIZE // CMP_STRIDE
    for p in range(n_pages):
        page_t = page_refs[p][0, 0].reshape(KV_COLS, PAGE_SIZE).astype(BF16)
        xs_ref[p] = _dot_nt(perm_ref[...], page_t)
    lhs = jnp.concatenate(
        [xs_ref[:, l * per_page:(l + 1) * per_page, :].reshape(n_pages * per_page, KV_COLS).astype(BF16)
         for l in range(CMP_STRIDE)], axis=1)
    ac_ref[0] = _dot(lhs, w1_ref[...])


def _compress_sample(cache_t, page_table, w, layer):
    b, n_pg = page_table.shape
    p = min(PAGES_PER_STEP, n_pg)
    n = p * PAGE_SIZE // CMP_STRIDE
    w1 = w["phi1_rows"]
    rows = np.arange(PAGE_SIZE)
    perm = np.zeros((PAGE_SIZE, PAGE_SIZE), np.float32)
    perm[rows, (rows % (PAGE_SIZE // CMP_STRIDE)) * CMP_STRIDE + rows // (PAGE_SIZE // CMP_STRIDE)] = 1.0
    perm = jnp.asarray(perm, BF16)

    def page_spec(j):
        return pl.BlockSpec((1, 1, N_KV_HEADS, 2, HEAD_DIM, PAGE_SIZE),
                            lambda bi, c, pt: (layer, pt[bi, c * p + j], 0, 0, 0, 0))

    return pl.pallas_call(
        functools.partial(_compress_sample_kernel, n_pages=p),
        grid_spec=pltpu.PrefetchScalarGridSpec(
            num_scalar_prefetch=1,
            grid=(b, n_pg // p),
            in_specs=[page_spec(j) for j in range(p)]
            + [pl.BlockSpec(perm.shape, lambda bi, c, pt: (0, 0)),
               pl.BlockSpec(w1.shape, lambda bi, c, pt: (0, 0))],
            out_specs=pl.BlockSpec((1, n, 2 * KV_COLS), lambda bi, c, pt: (bi, c, 0)),
            scratch_shapes=[pltpu.VMEM((p, PAGE_SIZE, KV_COLS), F32)],
        ),
        out_shape=jax.ShapeDtypeStruct((b, n_pg * PAGE_SIZE // CMP_STRIDE, 2 * KV_COLS), F32),
        compiler_params=_cparams(("parallel", "parallel")),
        name="compress_sample",
    )(page_table, *([cache_t] * p), perm, w1)


def _finish_sample_kernel(ac_ref, w2_ref, kvt_ref, res_ref):
    res = _finish_compress(ac_ref[0], w2_ref[...])
    n_sel = res.shape[0] // 4
    for kvh in range(N_KV_HEADS):
        res_ref[kvh] = res[:, kvh * LANES:(kvh + 1) * LANES]
        for r in range(4):
            kv = res_ref[kvh, pl.ds(r, n_sel, stride=4), :]
            kvt_ref[0, kvh, :, r * n_sel:(r + 1) * n_sel] = kv.T.astype(kvt_ref.dtype)


def _finish_sample(ac, w):
    b, n_cmp, _ = ac.shape
    return pl.pallas_call(
        _finish_sample_kernel,
        grid=(b,),
        in_specs=[pl.BlockSpec((1, n_cmp, 2 * KV_COLS), lambda i: (i, 0, 0)), _full((KV_COLS, KV_COLS))],
        out_specs=pl.BlockSpec((1, N_KV_HEADS, LANES, n_cmp), lambda i: (i, 0, 0, 0)),
        out_shape=jax.ShapeDtypeStruct((b, N_KV_HEADS, LANES, n_cmp), BF16),
        scratch_shapes=[pltpu.VMEM((N_KV_HEADS, n_cmp, LANES), F32)],
        compiler_params=_cparams(("parallel",)),
        name="finish_compress_sample",
    )(ac, w["phi2_big"])


SEQS_PER_STEP = 4


def _cmp_select_sample_kernel(q_ref, kvt_ref, cpos_ref, slope_ref, oc_ref, idx_ref, *, n_sel, t_past):
    n_cmp = 4 * n_sel
    n_seq = q_ref.shape[0]
    rows = n_seq * N_HEADS
    group0 = lax.broadcasted_iota(jnp.int32, (N_HEADS, n_cmp), 0) < GROUP
    s = []
    for b in range(n_seq):
        q = q_ref[b].astype(BF16)
        s.append(jnp.where(group0, _dot(q, kvt_ref[b, 0, :HEAD_DIM, :]), _dot(q, kvt_ref[b, 1, :HEAD_DIM, :])))
    s = jnp.concatenate(s, axis=0)
    dist = jnp.broadcast_to(float(t_past) - cpos_ref[...], (rows, n_cmp))
    pr = _cmp_probs(s, dist, dist >= 0, slope_ref[:, 0:1])
    prb = pr.astype(BF16)
    imp = []
    for b in range(n_seq):
        rb = slice(b * N_HEADS, (b + 1) * N_HEADS)
        oc_ref[b] = jnp.where(group0[:, :HEAD_DIM], _dot_nt(prb[rb], kvt_ref[b, 0, HEAD_DIM:, :]),
                              _dot_nt(prb[rb], kvt_ref[b, 1, HEAD_DIM:, :]))
        imp.append(jnp.where(group0, jnp.sum(pr[rb][:GROUP], axis=0, keepdims=True),
                             jnp.sum(pr[rb][GROUP:], axis=0, keepdims=True)))
    imp = jnp.concatenate(imp, axis=0)
    pooled = imp[:, :n_sel]
    for r in range(1, 4):
        pooled = pooled + imp[:, r * n_sel:(r + 1) * n_sel]
    pooled = jnp.concatenate([pooled, jnp.zeros((rows, LANES), F32)], axis=1)
    cur = jnp.full(pooled.shape, t_past // SLC_BLOCK, jnp.int32)
    _, picks = _select_blocks(pooled, cur, pooled.shape[1])
    lane = lax.broadcasted_iota(jnp.int32, (rows, LANES), 1)
    idx = jnp.zeros((rows, LANES), F32)
    for it, pick in enumerate(picks):
        idx = jnp.where(lane == it, pick, idx)
    idx_ref[...] = idx.astype(jnp.int32).reshape(idx_ref.shape)


def _cmp_select_sample(q8, kvt_cmp, slopes, *, t_past):
    b = q8.shape[0]
    g = SEQS_PER_STEP
    n_cmp = kvt_cmp.shape[-1]
    n_sel = n_cmp // 4
    col = jnp.arange(n_cmp)
    cpos = ((4 * (col % n_sel) + col // n_sel) * CMP_STRIDE + (CMP_LEN - 1)).astype(F32)[None, :]
    slopes = jnp.tile(slopes, (g, 1))
    return pl.pallas_call(
        functools.partial(_cmp_select_sample_kernel, n_sel=n_sel, t_past=t_past),
        grid=(b // g,),
        in_specs=[pl.BlockSpec((g, N_HEADS, HEAD_DIM), lambda i: (i, 0, 0)),
                  pl.BlockSpec((g, N_KV_HEADS, LANES, n_cmp), lambda i: (i, 0, 0, 0)),
                  _full(cpos.shape), _full(slopes.shape)],
        out_specs=(pl.BlockSpec((g, N_HEADS, HEAD_DIM), lambda i: (i, 0, 0)),
                   pl.BlockSpec((g, N_HEADS, LANES), lambda i: (i, 0, 0))),
        out_shape=(jax.ShapeDtypeStruct((b, N_HEADS, HEAD_DIM), F32),
                   jax.ShapeDtypeStruct((b, N_HEADS, LANES), jnp.int32)),
        compiler_params=_cparams(("parallel",)),
        name="cmp_select_sample",
    )(q8, kvt_cmp, cpos, slopes)


def _attend_pieces(pieces, q, k_rows, v_rows):
    s_new = jnp.sum(q * k_rows, axis=-1, keepdims=True)
    masked = [jnp.where(valid, s, MASKED) for s, valid, _ in pieces]
    m = s_new
    for s in masked:
        m = jnp.maximum(m, jnp.max(s, axis=-1, keepdims=True))
    p_new = jnp.exp(s_new - m)
    l, acc = p_new, p_new * v_rows
    for s, (_, valid, vt) in zip(masked, pieces):
        p = jnp.where(valid, jnp.exp(s - m), 0.0)
        l = l + jnp.sum(p, axis=-1, keepdims=True)
        acc = acc + _dot_nt(p.astype(BF16), vt)
    return acc / jnp.maximum(l, 1e-30)


def _slc_win_sample_kernel(sel_ref, pt_ref, q_ref, *refs, n_top, t_past):
    n_pages = N_KV_HEADS * n_top
    page_refs = refs[:n_pages]
    win_ref, snew_ref, wnew_ref, gates_ref, oc_ref, slope_ref, o_ref = refs[n_pages:]
    b = pl.program_id(0)
    q = q_ref[0]
    qb = q.astype(BF16)
    slope = slope_ref[:, 0:1]
    row_kvh = lax.shift_right_logical(lax.broadcasted_iota(jnp.int32, (N_HEADS, LANES), 0), 2)
    lane = lax.broadcasted_iota(jnp.int32, (N_HEADS, LANES), 1)
    lane_half = lax.shift_right_logical(lane, 6)
    pieces = []
    for kvh in range(N_KV_HEADS):
        for j in range(n_top):
            blk = sel_ref[b, kvh * n_top + j]
            page = page_refs[kvh * n_top + j]
            kpos = lax.shift_right_logical(blk, 1) * PAGE_SIZE + lane
            dist = (t_past - kpos).astype(F32)
            valid = (lane_half == (blk & 1)) & (row_kvh == kvh) & (blk < t_past // SLC_BLOCK)
            s = _dot(qb, page[0, 0, 0, 0].astype(BF16)) - slope * dist
            pieces.append((s, valid, page[0, 0, 0, 1].astype(BF16)))
    o_slc = _attend_pieces(pieces, q, snew_ref[0, 0], snew_ref[0, 1])

    pieces = []
    w = win_ref.shape[-1]
    wlane = lax.broadcasted_iota(jnp.int32, (N_HEADS, w), 1)
    wdist = (w - wlane).astype(F32)
    wrow_kvh = lax.shift_right_logical(lax.broadcasted_iota(jnp.int32, (N_HEADS, w), 0), 2)
    for kvh in range(N_KV_HEADS):
        s = _dot(qb, win_ref[0, 0, kvh, 0].astype(BF16)) - slope * wdist
        valid = (wrow_kvh == kvh) & (wdist <= WINDOW)
        pieces.append((s, valid, win_ref[0, 0, kvh, 1].astype(BF16)))
    o_win = _attend_pieces(pieces, q, wnew_ref[0, 0], wnew_ref[0, 1])

    o_ref[0] = gates_ref[0, 0] * oc_ref[0] + gates_ref[0, 1] * o_slc + gates_ref[0, 2] * o_win


def _slc_win_sample(sel, page_table, q8, slc_t, win_t, slc_new, win_new, gates, o_cmp, slopes, layer, *,
                    t_past):
    b, n_pg = page_table.shape
    n_top = sel.shape[1] // N_KV_HEADS
    w = win_t.shape[-1]

    def page_spec(kvh, j):
        def index(bi, sel_ref, pt_ref):
            pg = jnp.minimum(lax.shift_right_logical(sel_ref[bi, kvh * n_top + j], 1), n_pg - 1)
            return (layer, pt_ref[bi, pg], kvh, 0, 0, 0)
        return pl.BlockSpec((1, 1, 1, 2, HEAD_DIM, PAGE_SIZE), index)

    per_seq = lambda *tail: pl.BlockSpec((1,) + tail, lambda bi, s, p: (bi,) + (0,) * len(tail))
    return pl.pallas_call(
        functools.partial(_slc_win_sample_kernel, n_top=n_top, t_past=t_past),
        grid_spec=pltpu.PrefetchScalarGridSpec(
            num_scalar_prefetch=2,
            grid=(b,),
            in_specs=[per_seq(N_HEADS, HEAD_DIM)]
            + [page_spec(kvh, j) for kvh in range(N_KV_HEADS) for j in range(n_top)]
            + [pl.BlockSpec((1, 1, N_KV_HEADS, 2, HEAD_DIM, w), lambda bi, s, p: (layer, bi, 0, 0, 0, 0)),
               per_seq(2, N_HEADS, HEAD_DIM), per_seq(2, N_HEADS, HEAD_DIM), per_seq(3, N_HEADS, 1),
               per_seq(N_HEADS, HEAD_DIM), pl.BlockSpec(slopes.shape, lambda bi, s, p: (0, 0))],
            out_specs=per_seq(N_HEADS, HEAD_DIM),
        ),
        out_shape=jax.ShapeDtypeStruct((b, N_HEADS, HEAD_DIM), F32),
        compiler_params=_cparams(("parallel",)),
        name="slc_win_sample",
    )(sel, page_table, q8, *([slc_t] * (N_KV_HEADS * n_top)), win_t, slc_new, win_new, gates, o_cmp, slopes)


MOE_TOKENS = 256
MOE_SLOTS = 256


def _n_slot_blocks(n_tokens):
    return -(-(2 * n_tokens + N_EXPERTS * (MOE_SLOTS - 1)) // MOE_SLOTS)


def _route_kernel(x_ref, wr_ref, dest_ref, wts_ref, blk_ref, top_ref, count_ref, base_ref, *, n_blocks):
    phase, i = pl.program_id(0), pl.program_id(1)
    tt = x_ref.shape[0]
    rowf = lax.broadcasted_iota(jnp.int32, (N_EXPERTS, tt), 0).astype(F32)
    r8 = lax.broadcasted_iota(jnp.int32, (8, tt), 0)

    @pl.when((phase == 0) & (i == 0))
    def _():
        count_ref[...] = jnp.zeros(count_ref.shape, F32)

    @pl.when(phase == 0)
    def _():
        logits = lax.dot_general(wr_ref[...], x_ref[...], (((1,), (1,)), ((), ())),
                                 precision=lax.Precision.HIGHEST, preferred_element_type=F32)
        m1 = jnp.max(logits, axis=0, keepdims=True)
        e1 = jnp.min(jnp.where(logits == m1, rowf, float(N_EXPERTS)), axis=0, keepdims=True)
        rest = jnp.where(rowf == e1, NEG_INF, logits)
        m2 = jnp.max(rest, axis=0, keepdims=True)
        e2 = jnp.min(jnp.where(rest == m2, rowf, float(N_EXPERTS)), axis=0, keepdims=True)
        top_ref[i] = jnp.where(r8 == 0, e1, jnp.where(r8 == 1, e2, jnp.where(r8 == 2, m1, m2)))
        onehot = jnp.where((rowf == e1) | (rowf == e2), 1.0, 0.0)
        count_ref[...] += jnp.sum(onehot, axis=1, keepdims=True)

    top = top_ref[i]
    e1, e2, m1, m2 = top[0:1], top[1:2], top[2:3], top[3:4]
    hot1, hot2 = rowf == e1, rowf == e2
    onehot = jnp.where(hot1 | hot2, 1.0, 0.0)
    tile_count = jnp.sum(onehot, axis=1, keepdims=True)

    @pl.when((phase == 1) & (i == 0))
    def _():
        padded = jnp.ceil(count_ref[...] / MOE_SLOTS) * MOE_SLOTS
        erow = lax.broadcasted_iota(jnp.int32, padded.shape, 0)
        start = jnp.zeros(padded.shape, F32)
        end = jnp.zeros(padded.shape, F32)
        for e in range(N_EXPERTS):
            pe = padded[e:e + 1, :]
            start = start + jnp.where(erow > e, pe, 0.0)
            end = end + jnp.where(erow >= e, pe, 0.0)
        base_ref[...] = start
        first_row = lax.broadcasted_iota(jnp.int32, (N_EXPERTS, blk_ref.shape[1]), 1).astype(F32) * MOE_SLOTS
        owner = jnp.sum(jnp.where(end[:, 0:1] <= first_row, 1.0, 0.0), axis=0, keepdims=True)
        used = jnp.max(end[:, 0:1], axis=0, keepdims=True) / MOE_SLOTS
        r8 = lax.broadcasted_iota(jnp.int32, blk_ref.shape, 0)
        blk = jnp.where(r8 == 0, jnp.minimum(owner, N_EXPERTS - 1.0), jnp.broadcast_to(used, blk_ref.shape))
        blk_ref[...] = blk.astype(jnp.int32)

    @pl.when(phase == 1)
    def _():
        upper = (lax.broadcasted_iota(jnp.int32, (tt, tt), 0)
                 < lax.broadcasted_iota(jnp.int32, (tt, tt), 1))
        before = _dot(onehot.astype(BF16), jnp.where(upper, 1.0, 0.0).astype(BF16))
        slot = base_ref[:, 0:1] + before
        d1 = jnp.sum(jnp.where(hot1, slot, 0.0), axis=0, keepdims=True)
        d2 = jnp.sum(jnp.where(hot2, slot, 0.0), axis=0, keepdims=True)
        dest_ref[0] = jnp.where(r8 == 0, d1, d2).astype(jnp.int32)
        z = jnp.exp(m2 - m1)
        w1 = 1.0 / (1.0 + z)
        wts_ref[0] = jnp.where(r8 == 0, w1, z * w1)
        base_ref[...] += tile_count


def _route(x, w_router_t):
    n = x.shape[0]
    nt = n // MOE_TOKENS
    n_blocks = _n_slot_blocks(n)
    blk_lanes = -(-n_blocks // LANES) * LANES
    tile = lambda ph, i: (i * ph, 0, 0)
    return pl.pallas_call(
        functools.partial(_route_kernel, n_blocks=n_blocks),
        grid=(2, nt),
        in_specs=[pl.BlockSpec((MOE_TOKENS, D_MODEL), lambda ph, i: (i * (1 - ph) + (nt - 1) * ph, 0)),
                  pl.BlockSpec(w_router_t.shape, lambda ph, i: (0, 0))],
        out_specs=(pl.BlockSpec((1, 8, MOE_TOKENS), tile), pl.BlockSpec((1, 8, MOE_TOKENS), tile),
                   pl.BlockSpec((8, blk_lanes), lambda ph, i: (0, 0))),
        out_shape=(jax.ShapeDtypeStruct((nt, 8, MOE_TOKENS), jnp.int32),
                   jax.ShapeDtypeStruct((nt, 8, MOE_TOKENS), F32),
                   jax.ShapeDtypeStruct((8, blk_lanes), jnp.int32)),
        scratch_shapes=[pltpu.VMEM((nt, 8, MOE_TOKENS), F32), pltpu.VMEM((N_EXPERTS, LANES), F32),
                        pltpu.VMEM((N_EXPERTS, LANES), F32)],
        compiler_params=_cparams(("arbitrary", "arbitrary")),
        name="moe_route",
    )(x, w_router_t)


def _row_copy(src, src_row, dst, dst_row, sem):
    return pltpu.make_async_copy(src.at[pl.ds(src_row, 1), :], dst.at[pl.ds(dst_row, 1), :], sem)


ISSUE_UNROLL = 8


def _drain_rows(src, dst, sem, n):
    def drain(r, carry):
        _row_copy(src, 0, dst, 0, sem).wait()
        return carry
    lax.fori_loop(0, n, drain, 0, unroll=ISSUE_UNROLL)


def _scatter_kernel(d1_ref, d2_ref, x_ref, init_ref, xs_ref, stage_ref, sem):
    del init_ref
    i, n = pl.program_id(0), pl.num_programs(0)
    tt = x_ref.shape[0]
    slot = i % 2

    @pl.when(i >= 2)
    def _():
        _drain_rows(stage_ref.at[slot], xs_ref, sem.at[slot], 2 * tt)

    stage_ref[slot] = x_ref[...]

    def issue(r, carry):
        t = i * tt + r
        _row_copy(stage_ref.at[slot], r, xs_ref, d1_ref[t], sem.at[slot]).start()
        _row_copy(stage_ref.at[slot], r, xs_ref, d2_ref[t], sem.at[slot]).start()
        return carry

    lax.fori_loop(0, tt, issue, 0, unroll=ISSUE_UNROLL)

    @pl.when(i == n - 1)
    def _():
        _drain_rows(stage_ref.at[slot], xs_ref, sem.at[slot], 2 * tt)

        @pl.when(n >= 2)
        def _():
            _drain_rows(stage_ref.at[1 - slot], xs_ref, sem.at[1 - slot], 2 * tt)


def _scatter_rows(x, d1, d2, n_slots):
    n = x.shape[0]
    init = jnp.zeros((n_slots, D_MODEL), F32)
    return pl.pallas_call(
        _scatter_kernel,
        grid_spec=pltpu.PrefetchScalarGridSpec(
            num_scalar_prefetch=2,
            grid=(n // MOE_TOKENS,),
            in_specs=[pl.BlockSpec((MOE_TOKENS, D_MODEL), lambda i, a, b: (i, 0)),
                      pl.BlockSpec(memory_space=pl.ANY)],
            out_specs=pl.BlockSpec(memory_space=pl.ANY),
            scratch_shapes=[pltpu.VMEM((2, MOE_TOKENS, D_MODEL), F32), pltpu.SemaphoreType.DMA((2,))],
        ),
        out_shape=jax.ShapeDtypeStruct((n_slots, D_MODEL), F32),
        input_output_aliases={3: 0},
        compiler_params=_cparams(("arbitrary",)),
        name="moe_scatter",
    )(d1, d2, x, init)


FF_HALF = D_FF // 2


def _expert_ffn_kernel(blk_ref, x_ref, w1_ref, w2_ref, y_ref):
    j = pl.program_id(0)

    @pl.when(j < blk_ref[1, 0])
    def _():
        x = x_ref[...].astype(BF16)
        y = jnp.zeros(y_ref.shape, F32)
        for c in range(2):
            g = _dot(x, w1_ref[0, :, c * FF_HALF:(c + 1) * FF_HALF])
            u = _dot(x, w1_ref[0, :, D_FF + c * FF_HALF:D_FF + (c + 1) * FF_HALF])
            act = (jax.nn.silu(g) * u).astype(BF16)
            y = y + _dot(act, w2_ref[0, c * FF_HALF:(c + 1) * FF_HALF, :])
        y_ref[...] = y

    @pl.when(j >= blk_ref[1, 0])
    def _():
        y_ref[...] = jnp.zeros(y_ref.shape, F32)


def _expert_ffn(xs, blk, w1, w2):
    n_slots = xs.shape[0]
    return pl.pallas_call(
        _expert_ffn_kernel,
        grid_spec=pltpu.PrefetchScalarGridSpec(
            num_scalar_prefetch=1,
            grid=(n_slots // MOE_SLOTS,),
            in_specs=[pl.BlockSpec((MOE_SLOTS, D_MODEL), lambda j, b: (j, 0)),
                      pl.BlockSpec((1, D_MODEL, 2 * D_FF), lambda j, b: (b[0, j], 0, 0)),
                      pl.BlockSpec((1, D_FF, D_MODEL), lambda j, b: (b[0, j], 0, 0))],
            out_specs=pl.BlockSpec((MOE_SLOTS, D_MODEL), lambda j, b: (j, 0)),
        ),
        out_shape=jax.ShapeDtypeStruct((n_slots, D_MODEL), F32),
        compiler_params=_cparams(("arbitrary",)),
        name="moe_expert_ffn",
    )(blk, xs, w1, w2)


def _combine_kernel(d1_ref, d2_ref, ys_ref, x_ref, wts_ref, g_ref, b_ref, o_ref, tail_ref, buf_ref, sem, *,
                    alpha):
    i, n = pl.program_id(0), pl.num_programs(0)
    tt = x_ref.shape[0]
    slot = i % 2

    def gather(tile, slot):
        def issue(r, carry):
            t = tile * tt + r
            _row_copy(ys_ref, d1_ref[t], buf_ref.at[slot, 0], r, sem.at[slot]).start()
            _row_copy(ys_ref, d2_ref[t], buf_ref.at[slot, 1], r, sem.at[slot]).start()
            return carry
        lax.fori_loop(0, tt, issue, 0, unroll=ISSUE_UNROLL)

    @pl.when(i == 0)
    def _():
        gather(0, 0)

    @pl.when(i + 1 < n)
    def _():
        gather(i + 1, 1 - slot)

    _drain_rows(ys_ref, buf_ref.at[slot, 0], sem.at[slot], 2 * tt)
    wcol = wts_ref[0].T
    y = wcol[:, 0:1] * buf_ref[slot, 0] + wcol[:, 1:2] * buf_ref[slot, 1]
    out = _ln(alpha * x_ref[...] + y, g_ref[...], b_ref[...])

    @pl.when(i < n - 1)
    def _():
        o_ref[...] = out

    @pl.when(i == n - 1)
    def _():
        tail_ref[...] = out


def _combine(ys, x, d1, d2, wts, g, b, *, alpha):
    n = x.shape[0]
    nt = n // MOE_TOKENS
    row = pl.BlockSpec((MOE_TOKENS, D_MODEL), lambda i, p, q: (i, 0))
    vec = pl.BlockSpec((1, D_MODEL), lambda i, p, q: (0, 0))
    return pl.pallas_call(
        functools.partial(_combine_kernel, alpha=alpha),
        grid_spec=pltpu.PrefetchScalarGridSpec(
            num_scalar_prefetch=2,
            grid=(nt,),
            in_specs=[pl.BlockSpec(memory_space=pl.ANY), row,
                      pl.BlockSpec((1, 8, MOE_TOKENS), lambda i, p, q: (i, 0, 0)), vec, vec],
            out_specs=(pl.BlockSpec((MOE_TOKENS, D_MODEL), lambda i, p, q: (jnp.minimum(i, nt - 2), 0)),
                       pl.BlockSpec((MOE_TOKENS, D_MODEL), lambda i, p, q: (0, 0))),
            scratch_shapes=[pltpu.VMEM((2, 2, MOE_TOKENS, D_MODEL), F32), pltpu.SemaphoreType.DMA((2,))],
        ),
        out_shape=(jax.ShapeDtypeStruct((n - MOE_TOKENS, D_MODEL), F32),
                   jax.ShapeDtypeStruct((MOE_TOKENS, D_MODEL), F32)),
        compiler_params=_cparams(("arbitrary",)),
        name="moe_combine",
    )(d1, d2, ys, x, wts, g, b)


def _moe_ffn(x, w_router_t, w1, w2, g, b, *, alpha):
    n = x.shape[0]
    dest, wts, blk = _route(x, w_router_t)
    d1, d2 = dest[:, 0, :].reshape(n), dest[:, 1, :].reshape(n)
    xs = _scatter_rows(x, d1, d2, _n_slot_blocks(n) * MOE_SLOTS)
    ys = _expert_ffn(xs, blk, w1, w2)
    return _combine(ys, x, d1, d2, wts, g, b, alpha=alpha)


def _prep_layer(l, w_in, w_phi1, w_phi2, gmlp_ln_g, gmlp_ln_b, gmlp_ws, gmlp_bs, w_out, ln1_g, ln1_b):
    wi = w_in[l]
    c0 = NSA_WIDTH
    c1, c2, c3 = c0 + KV_COLS, c0 + 2 * KV_COLS, c0 + 3 * KV_COLS
    c4 = c3 + N_GATES
    c5 = c4 + GMLP_WIDTH
    eye = jnp.eye(N_KV_HEADS, dtype=F32)
    w1 = w_phi1[l]
    wa = jnp.einsum("cldf,kK,cC->lkcdKCf", w1[:, :CMP_STRIDE], eye, eye)
    wc = jnp.einsum("cldf,kK,cC->lkcdKCf", w1[:, CMP_STRIDE:], eye, eye)
    rows = CMP_STRIDE * KV_COLS
    phi1_rows = jnp.concatenate([wa.reshape(rows, KV_COLS), wc.reshape(rows, KV_COLS)], axis=1)
    phi2_big = jnp.einsum("cfe,kK,cC->kcfKCe", w_phi2[l], eye, eye).reshape(KV_COLS, KV_COLS)
    ws, bs = gmlp_ws[l], gmlp_bs[l]
    g_bias = jnp.repeat(bs.reshape(GMLP_WIDTH // LANES, 2, CHUNK).transpose(0, 2, 1), GMLP_GROUP_DIM, axis=2)
    return {
        "wq": (wi[:, :c0] * SCALE).astype(BF16),
        "wkvt": wi[:, c0:c3].T.astype(BF16),
        "wkvc": wi[:, c0:c1].astype(BF16),
        "wg": jnp.pad(wi[:, c3:c4], ((0, 0), (0, LANES - N_GATES))).astype(BF16),
        "wu": wi[:, c4:c5].astype(BF16),
        "wv": wi[:, c5:].astype(BF16),
        "lng": gmlp_ln_g[l][None, :],
        "lnb": gmlp_ln_b[l][None, :],
        "g_tril": jnp.tril(ws).astype(BF16),
        "g_bias": g_bias,
        "g_diag_w": jnp.repeat(ws[:, 0, 0], GMLP_GROUP_DIM)[None, :],
        "g_diag_b": jnp.repeat(bs[:, 0], GMLP_GROUP_DIM)[None, :],
        "phi1_rows": phi1_rows.astype(BF16),
        "phi2_big": phi2_big.astype(BF16),
        "w_phi1": w1,
        "wo_a": w_out[l][:NSA_WIDTH].astype(BF16),
        "wo_g": w_out[l][NSA_WIDTH:].astype(BF16),
        "ln1_g": ln1_g[l][None, :],
        "ln1_b": ln1_b[l][None, :],
    }


def _kv_leaf(kvt):
    return kvt.reshape(N_KV_HEADS, 2, HEAD_DIM, kvt.shape[-1]).transpose(3, 0, 1, 2)


def _mix_prompt(xp, w, alpha, out_rows=None):
    q, kvt, kvtb, kvc_rows, gates, og, _ = _inproj(xp, w, sample=False, tm=256)
    cmp_kv = _compress_prompt(kvc_rows, w)
    o_cmp, unsel, flags = _cmp_select(q, cmp_kv, tq=128)
    o_attn = _slc_win(q, kvtb, unsel, flags, gates, o_cmp, tq=128)
    x1 = _outproj(o_attn, og, xp, w, alpha=alpha, tm=256, name="outproj_prompt", out_rows=out_rows)
    return x1, kvt


def _cache_view(cache):
    return cache.transpose(0, 1, 3, 4, 5, 2)


def _mix_sample(xs, w, alpha, cmp_t, slc_t, win_t, page_table, layer, append_to=None):
    b = xs.shape[0]
    t_past = page_table.shape[1] * PAGE_SIZE
    q, kvt, _, _, gates, og, vn = _inproj(xs, w, sample=True, tm=b)
    kvt_cmp = _finish_sample(_compress_sample(cmp_t, page_table, w, layer), w)
    q8 = q.astype(F32).reshape(b, N_HEADS, HEAD_DIM)
    kv_new = kvt.T.reshape(b, 3, N_KV_HEADS, 2, HEAD_DIM)
    per_head = lambda kv: jnp.repeat(kv.transpose(0, 2, 1, 3), GROUP, axis=2)
    slopes = jnp.broadcast_to(jnp.exp2(-jnp.arange(1, N_HEADS + 1, dtype=F32))[:, None], (N_HEADS, LANES))
    o_cmp, idx = _cmp_select_sample(q8, kvt_cmp, slopes, t_past=t_past)
    sel = idx[:, ::GROUP, :N_SEL_BLOCKS].reshape(b, N_KV_HEADS * N_SEL_BLOCKS)
    gates8 = gates[:, :N_GATES].reshape(b, 3, N_HEADS, 1)
    o_attn = _slc_win_sample(sel, page_table, q8, slc_t, win_t, per_head(kv_new[:, 1]), per_head(kv_new[:, 2]),
                             gates8, o_cmp, slopes, layer, t_past=t_past)
    o_attn = o_attn.reshape(b, NSA_WIDTH).astype(BF16)
    x1 = _outproj(o_attn, og, xs, w, alpha=alpha, tm=b, name="outproj_sample", append_to=append_to)
    return x1, kv_new, vn


def kernel(x_prompt, x_sample, cache_cmp, cache_slc, cache_win, page_table, w_in, w_phi1, w_phi2, gmlp_ln_g,
           gmlp_ln_b, gmlp_ws, gmlp_bs, w_out, ln1_g, ln1_b, ln2_g, ln2_b, ffn_w1, ffn_w2, moe_router,
           moe_w1, moe_w2):
    depth = w_in.shape[0]
    alpha = (2 * depth) ** 0.25
    t, b = x_prompt.shape[1], x_sample.shape[0]
    xp, xs = x_prompt[0], x_sample[:, 0]
    cmp_t, slc_t, win_t = _cache_view(cache_cmp), _cache_view(cache_slc), _cache_view(cache_win)
    n_win = min(WINDOW, t)
    outs = [[] for _ in range(7)]
    for l in range(depth):
        w = _prep_layer(l, w_in, w_phi1, w_phi2, gmlp_ln_g, gmlp_ln_b, gmlp_ws, gmlp_bs, w_out, ln1_g, ln1_b)
        moe = l % 2 == 1
        x1p, kvt = _mix_prompt(xp, w, alpha, out_rows=t + MOE_TOKENS if moe else None)
        x1s, kv_new, vn = _mix_sample(xs, w, alpha, cmp_t, slc_t, win_t, page_table, l,
                                      append_to=x1p if moe else None)
        outs[0].append(_kv_leaf(kvt[:KV_COLS])[None])
        outs[1].append(_kv_leaf(kvt[KV_COLS:2 * KV_COLS])[None])
        outs[2].append(_kv_leaf(kvt[2 * KV_COLS:, t - n_win:])[None])
        for k in range(3):
            outs[3 + k].append(kv_new[:, k][:, None])
        outs[6].append(vn[:, None, :])
        g2, b2 = ln2_g[l][None, :], ln2_b[l][None, :]
        i = l // 2
        if not moe:
            w1, w2 = ffn_w1[i].astype(BF16), ffn_w2[i].astype(BF16)
            xp = _ffn_dense(x1p, w1, w2, g2, b2, alpha=alpha, tm=256, name="ffn_prompt")
            xs = _ffn_dense(x1s, w1, w2, g2, b2, alpha=alpha, tm=b, name="ffn_sample")
        else:
            xp, tail = _moe_ffn(x1s, moe_router[i].T, moe_w1[i].astype(BF16), moe_w2[i].astype(BF16), g2, b2,
                                alpha=alpha)
            xs = tail[:b]
    return (xp[None], xs[:, None]) + tuple(jnp.stack(o) for o in outs)
```
